```python
import jax
import jax.numpy as jnp
from jax import lax
import numpy as np

D_MODEL = 4096
BATCH = 2
SEQ = 4096
DEPTH = 2

MLA_HEADS = 16
MLA_NOPE = 128
MLA_ROPE = 64
MLA_V = 128
MLA_Q_RANK = 1024
MLA_KV_RANK = 512
MLA_QK = MLA_NOPE + MLA_ROPE
MLA_WIDTH = MLA_HEADS * MLA_V
Q_BLOCK = 128
MASK_VALUE = -1e30
HG_HEADS = 16
HG_DK = 128
HG_DV = 128
HG_KEY_WIDTH = HG_HEADS * HG_DK
HG_WIDTH = HG_HEADS * HG_DV
HG_CHUNK = 64
HG_MIN_F = 1e-6
RET_HEADS = 8
RET_DK = 128
RET_DV = 256
RET_KEY_WIDTH = RET_HEADS * RET_DK
RET_WIDTH = RET_HEADS * RET_DV
RET_CHUNK = 128
N_BRANCH = 3
BRANCH_WIDTH = 2048
ROPE_BASE = 10000.0
D_FF = 11008
N_EXPERTS = 8
TOP_K = 2
D_EXPERT = 3584
N_DENSE = (DEPTH + 1) // 2
N_MOE = DEPTH // 2
N_MOD = 6
EPS = 1e-6

IN_SIZES = (MLA_Q_RANK, MLA_KV_RANK, MLA_ROPE,
            HG_KEY_WIDTH, HG_KEY_WIDTH, HG_WIDTH, HG_WIDTH,
            RET_KEY_WIDTH, RET_KEY_WIDTH, RET_WIDTH, RET_WIDTH,
            N_BRANCH * D_MODEL)
D_IN = sum(IN_SIZES)
IN_SPLITS = tuple(int(v) for v in np.cumsum(IN_SIZES)[:-1])

kernel_name = 'hybrid_mla_hgrn2_retention_moe_block'


def rms_norm(x, g):
    xf = x.astype(jnp.float32)
    y = xf * lax.rsqrt(jnp.mean(xf * xf, axis=-1, keepdims=True) + EPS)
    return (y * g.astype(jnp.float32)).astype(x.dtype)


def modulate(h, shift, scale):
    return h * (1.0 + scale) + shift


def rope(x, pos):
    d = x.shape[-1]
    inv = ROPE_BASE ** (-jnp.arange(0, d, 2, dtype=jnp.float32) / d)
    ang = pos.astype(jnp.float32)[..., None] * inv
    ang = ang.reshape(pos.shape + (1,) * (x.ndim - 3) + (d // 2,))
    cos, sin = jnp.cos(ang), jnp.sin(ang)
    xf = x.astype(jnp.float32)
    x1, x2 = xf[..., : d // 2], xf[..., d // 2:]
    return jnp.concatenate([x1 * cos - x2 * sin, x2 * cos + x1 * sin], axis=-1).astype(x.dtype)


def to_chunks(t, n_heads, d, chunk):
    B, S = t.shape[0], t.shape[1]
    t = t.astype(jnp.float32).reshape(B, S // chunk, chunk, n_heads, d)
    return t.transpose(1, 0, 3, 2, 4)


def from_chunks(o):
    nc, B, H, C, d = o.shape
    return o.transpose(1, 0, 3, 2, 4).reshape(B, nc * C, H, d)


def mla(cq, ckv, kr, pos, q_norm, w_q_up, kv_norm, w_kv_up):
    B, S, _ = cq.shape
    q = (rms_norm(cq, q_norm) @ w_q_up).reshape(B, S, MLA_HEADS, MLA_QK)
    q_nope, q_rope = q[..., :MLA_NOPE], rope(q[..., MLA_NOPE:], pos)
    kv = (rms_norm(ckv, kv_norm) @ w_kv_up).reshape(B, S, MLA_HEADS, MLA_NOPE + MLA_V)
    k_nope, v = kv[..., :MLA_NOPE], kv[..., MLA_NOPE:]
    k_rope = rope(kr, pos)
    scale = MLA_QK ** -0.5
    nb = S // Q_BLOCK
    qn_b = q_nope.reshape(B, nb, Q_BLOCK, MLA_HEADS, MLA_NOPE).transpose(1, 0, 2, 3, 4)
    qr_b = q_rope.reshape(B, nb, Q_BLOCK, MLA_HEADS, MLA_ROPE).transpose(1, 0, 2, 3, 4)
    key_idx = jnp.arange(S)

    def block(args):
        qn, qr, i = args
        s = jnp.einsum('bqhd,bkhd->bhqk', qn, k_nope) + jnp.einsum('bqhr,bkr->bhqk', qr, k_rope)
        s = s.astype(jnp.float32) * scale
        q_idx = i * Q_BLOCK + jnp.arange(Q_BLOCK)
        s = jnp.where(key_idx[None, :] <= q_idx[:, None], s, MASK_VALUE)
        p = jax.nn.softmax(s, axis=-1).astype(v.dtype)
        return jnp.einsum('bhqk,bkhd->bqhd', p, v)

    o = lax.map(block, (qn_b, qr_b, jnp.arange(nb)))
    return o.transpose(1, 0, 2, 3, 4).reshape(B, S, MLA_WIDTH)


def hgrn2(q, f_raw, i, gate, lb, norm_g):
    B, S, _ = q.shape
    q = jax.nn.silu(q.astype(jnp.float32))
    lb = lb.astype(jnp.float32)
    fr = f_raw.astype(jnp.float32)
    f = lb + (1.0 - lb) * jax.nn.sigmoid(fr)
    log_f = jnp.log(jnp.maximum(f, HG_MIN_F))
    k = (1.0 - lb) * jax.nn.sigmoid(-fr)
    qc = to_chunks(q, HG_HEADS, HG_DK, HG_CHUNK)
    kc = to_chunks(k, HG_HEADS, HG_DK, HG_CHUNK)
    lc = to_chunks(log_f, HG_HEADS, HG_DK, HG_CHUNK)
    vc = to_chunks(i, HG_HEADS, HG_DV, HG_CHUNK)
    causal = jnp.tril(jnp.ones((HG_CHUNK, HG_CHUNK), dtype=bool))[None, None, :, :, None]

    def step(state, inp):
        qt, kt, lt, vt = inp
        b = jnp.cumsum(lt, axis=2)
        rel = jnp.where(causal, b[:, :, :, None, :] - b[:, :, None, :, :], 0.0)
        dec = jnp.where(causal, jnp.exp(rel), 0.0)
        scores = jnp.einsum('bhtd,bhsd,bhtsd->bhts', qt, kt, dec)
        o = jnp.einsum('bhts,bhse->bhte', scores, vt) + jnp.einsum('bhtd,bhde->bhte', qt * jnp.exp(b), state)
        b_last = b[:, :, -1:, :]
        state = jnp.exp(b_last[:, :, 0, :])[..., None] * state + jnp.einsum('bhsd,bhse->bhde', kt * jnp.exp(b_last - b), vt)
        return state, o

    s0 = jnp.zeros((B, HG_HEADS, HG_DK, HG_DV), jnp.float32)
    _, o = lax.scan(step, s0, (qc, kc, lc, vc))
    o = from_chunks(o).astype(gate.dtype)
    o = rms_norm(o, norm_g.reshape(HG_HEADS, HG_DV)).reshape(B, S, HG_WIDTH)
    return o * jax.nn.silu(gate)


def retention(q, k, v, gate, pos, norm_g):
    B, S, _ = q.shape
    C = RET_CHUNK
    q = rope(q.reshape(B, S, RET_HEADS, RET_DK), pos)
    k = rope(k.reshape(B, S, RET_HEADS, RET_DK), pos) * (RET_DK ** -0.5)
    log_gamma = jnp.log1p(-(2.0 ** (-5.0 - jnp.arange(RET_HEADS, dtype=jnp.float32))))
    idx = jnp.arange(C, dtype=jnp.float32)
    rel = idx[:, None] - idx[None, :]
    decay = jnp.where(rel >= 0, jnp.exp(log_gamma[:, None, None] * jnp.maximum(rel, 0.0)), 0.0)
    q_dec = jnp.exp(log_gamma[:, None] * (idx + 1.0))[..., None]
    k_dec = jnp.exp(log_gamma[:, None] * (C - 1.0 - idx))[..., None]
    chunk_dec = jnp.exp(log_gamma * C)[:, None, None]
    qc = to_chunks(q, RET_HEADS, RET_DK, C)
    kc = to_chunks(k, RET_HEADS, RET_DK, C)
    vc = to_chunks(v, RET_HEADS, RET_DV, C)

    def step(state, inp):
        qt, kt, vt = inp
        s = jnp.einsum('bhtd,bhsd->bhts', qt, kt) * decay
        o = jnp.einsum('bhts,bhse->bhte', s, vt) + jnp.einsum('bhtd,bhde->bhte', qt * q_dec, state)
        state = chunk_dec * state + jnp.einsum('bhsd,bhse->bhde', kt * k_dec, vt)
        return state, o

    s0 = jnp.zeros((B, RET_HEADS, RET_DK, RET_DV), jnp.float32)
    _, o = lax.scan(step, s0, (qc, kc, vc))
    o = from_chunks(o).astype(gate.dtype)
    o = rms_norm(o, norm_g.reshape(RET_HEADS, RET_DV)).reshape(B, S, RET_WIDTH)
    return o * jax.nn.silu(gate)


def token_mixer(h, pos, lb, w_in, q_norm, w_q_up, kv_norm, w_kv_up, hg_norm, ret_norm, w_branch, w_out):
    B, S, D = h.shape
    proj = h @ w_in
    (cq, ckv, kr, hq, hf, hi, hgate, rq, rk, rv, rgate, gates) = jnp.split(proj, IN_SPLITS, axis=-1)
    y_mla = mla(cq, ckv, kr, pos, q_norm, w_q_up, kv_norm, w_kv_up)
    y_hg = hgrn2(hq, hf, hi, hgate, lb, hg_norm)
    y_ret = retention(rq, rk, rv, rgate, pos, ret_norm)
    gates = jax.nn.sigmoid(gates.reshape(B, S, N_BRANCH, D))
    merged = (gates[:, :, 0] * (y_mla @ w_branch[0])
              + gates[:, :, 1] * (y_hg @ w_branch[1])
              + gates[:, :, 2] * (y_ret @ w_branch[2]))
    return merged @ w_out


def swiglu(h, w1, w3, w2):
    return (jax.nn.silu(h @ w1) * (h @ w3)) @ w2


def moe_ffn(h, router, w1, w3, w2):
    logits = (h @ router).astype(jnp.float32)
    top_v, top_i = lax.top_k(logits, TOP_K)
    top_w = jax.nn.softmax(top_v, axis=-1)
    combine = jnp.sum(jax.nn.one_hot(top_i, N_EXPERTS, dtype=jnp.float32) * top_w[..., None], axis=-2)
    combine = combine.astype(h.dtype)
    y = combine[..., 0:1] * swiglu(h, w1[0], w3[0], w2[0])
    for e in range(1, N_EXPERTS):
        y = y + combine[..., e:e + 1] * swiglu(h, w1[e], w3[e], w2[e])
    return y


def setup_inputs(seed: int = 0) -> dict:
    key = jax.random.key(seed)
    ks = jax.random.split(key, 25)
    f32 = jnp.float32

    def w(k, shape, fan_in, gain=1.0):
        return jax.random.normal(k, shape, f32) * (gain * fan_in ** -0.5)

    def gain_vec(k, shape):
        return 1.0 + 0.1 * jax.random.normal(k, shape, f32)

    x = jax.random.normal(ks[0], (BATCH, SEQ, D_MODEL), f32)
    c = jax.random.normal(ks[1], (BATCH, D_MODEL), f32)
    offs = jax.random.randint(ks[2], (BATCH, 1), 0, 1024, dtype=jnp.int32)
    positions = offs + jnp.arange(SEQ, dtype=jnp.int32)[None, :]
    return {
        'x': x,
        'c': c,
        'positions': positions,
        'w_ada': w(ks[3], (D_MODEL, N_MOD * D_MODEL), D_MODEL, 0.5),
        'b_ada': 0.02 * jax.random.normal(ks[4], (N_MOD * D_MODEL,), f32),
        'ada_table': 0.1 * jax.random.normal(ks[5], (DEPTH, N_MOD, D_MODEL), f32),
        'norm_g': gain_vec(ks[6], (DEPTH, 2, D_MODEL)),
        'w_in': w(ks[7], (DEPTH, D_MODEL, D_IN), D_MODEL),
        'mla_q_norm': gain_vec(ks[8], (DEPTH, MLA_Q_RANK)),
        'mla_w_q_up': w(ks[9], (DEPTH, MLA_Q_RANK, MLA_HEADS * MLA_QK), MLA_Q_RANK),
        'mla_kv_norm': gain_vec(ks[10], (DEPTH, MLA_KV_RANK)),
        'mla_w_kv_up': w(ks[11], (DEPTH, MLA_KV_RANK, MLA_HEADS * (MLA_NOPE + MLA_V)), MLA_KV_RANK),
        'hg_lb_logits': 0.5 * jax.random.normal(ks[12], (DEPTH, HG_KEY_WIDTH), f32),
        'hg_norm': gain_vec(ks[13], (DEPTH, HG_WIDTH)),
        'ret_norm': gain_vec(ks[14], (DEPTH, RET_WIDTH)),
        'w_branch': w(ks[15], (DEPTH, N_BRANCH, BRANCH_WIDTH, D_MODEL), BRANCH_WIDTH),
        'w_out': w(ks[16], (DEPTH, D_MODEL, D_MODEL), D_MODEL),
        'ffn_w1': w(ks[17], (N_DENSE, D_MODEL, D_FF), D_MODEL),
        'ffn_w3': w(ks[18], (N_DENSE, D_MODEL, D_FF), D_MODEL),
        'ffn_w2': w(ks[19], (N_DENSE, D_FF, D_MODEL), D_FF),
        'moe_router': w(ks[20], (N_MOE, D_MODEL, N_EXPERTS), D_MODEL),
        'moe_w1': w(ks[21], (N_MOE, N_EXPERTS, D_MODEL, D_EXPERT), D_MODEL),
        'moe_w3': w(ks[22], (N_MOE, N_EXPERTS, D_MODEL, D_EXPERT), D_MODEL),
        'moe_w2': w(ks[23], (N_MOE, N_EXPERTS, D_EXPERT, D_MODEL), D_EXPERT),
        'final_norm': gain_vec(ks[24], (D_MODEL,)),
    }


def reference(x, c, positions, w_ada, b_ada, ada_table, norm_g, w_in,
              mla_q_norm, mla_w_q_up, mla_kv_norm, mla_w_kv_up,
              hg_lb_logits, hg_norm, ret_norm, w_branch, w_out,
              ffn_w1, ffn_w3, ffn_w2, moe_router, moe_w1, moe_w3, moe_w2,
              final_norm):
    B, S, D = x.shape
    mod = (jax.nn.silu(c) @ w_ada + b_ada).reshape(B, N_MOD, D)
    lb_p = jax.nn.softmax(hg_lb_logits.astype(jnp.float32), axis=0)
    lower_bounds = jnp.cumsum(lb_p, axis=0) - lb_p[0]
    for l in range(DEPTH):
        m = mod + ada_table[l]
        shift1, scale1, gate1, shift2, scale2, gate2 = (m[:, j, None, :] for j in range(N_MOD))
        h = modulate(rms_norm(x, norm_g[l, 0]), shift1, scale1)
        x = x + gate1 * token_mixer(h, positions, lower_bounds[l], w_in[l],
                                    mla_q_norm[l], mla_w_q_up[l], mla_kv_norm[l], mla_w_kv_up[l],
                                    hg_norm[l], ret_norm[l], w_branch[l], w_out[l])
        h = modulate(rms_norm(x, norm_g[l, 1]), shift2, scale2)
        if l % 2 == 0:
            y = swiglu(h, ffn_w1[l // 2], ffn_w3[l // 2], ffn_w2[l // 2])
        else:
            y = moe_ffn(h, moe_router[l // 2], moe_w1[l // 2], moe_w3[l // 2], moe_w2[l // 2])
        x = x + gate2 * y
    return rms_norm(x, final_norm)
```

```python
import functools
import math

import jax
import jax.numpy as jnp
from jax import lax
from jax.experimental import pallas as pl
from jax.experimental.pallas import tpu as pltpu

F32 = jnp.float32
BF16 = jnp.bfloat16

V7X_LANES = 128
V7X_SUBLANES = 8
V7X_VMEM_BYTES = 64 * 1024 * 1024
V7X_VMEM_LIMIT = V7X_VMEM_BYTES - 8 * 1024 * 1024

EPS = 1e-6
ROPE_BASE = 10000.0
MASK_VALUE = -1e30

MLA_HEADS = 16
MLA_NOPE = 128
MLA_ROPE = 64
MLA_V = 128
MLA_Q_RANK = 1024
MLA_KV_RANK = 512
MLA_QK = MLA_NOPE + MLA_ROPE
MLA_QPAD = 2 * V7X_LANES

HG_HEADS = 16
HG_D = 128
HG_WIDTH = HG_HEADS * HG_D
HG_CHUNK = 64
HG_SUB = 8
HG_MIN_F = 1e-6

RET_HEADS = 8
RET_DK = 128
RET_DV = 256
RET_CHUNK = 128

N_BRANCH = 3
N_EXPERTS = 8
N_MOD = 6


def _params(semantics):
    return pltpu.CompilerParams(dimension_semantics=semantics, vmem_limit_bytes=V7X_VMEM_LIMIT)


def _rms(x, gain):
    return x * lax.rsqrt(jnp.mean(x * x, axis=-1, keepdims=True) + EPS) * gain


def _silu(x):
    return x * jax.nn.sigmoid(x)


def _dot_nt(a, b):
    return lax.dot_general(a, b, (((1,), (1,)), ((), ())), preferred_element_type=F32)


def _dot_tn(a, b):
    return jnp.dot(a.T, b, preferred_element_type=F32)


def _ada_kernel(c_ref, w_ref, b_ref, tab_ref, o_ref):
    s = _silu(c_ref[...]).astype(BF16)
    acc = jnp.dot(s, w_ref[...].astype(BF16), preferred_element_type=F32) + b_ref[...]
    for l in range(o_ref.shape[0]):
        o_ref[l] = acc + tab_ref[l]


def _ada(c, w_ada, b_ada, ada_table):
    B, D = c.shape
    depth = ada_table.shape[0]
    N = w_ada.shape[1]
    tn = 1024
    cp = jnp.zeros((V7X_SUBLANES, D), F32).at[:B].set(c)
    out = pl.pallas_call(
        _ada_kernel,
        out_shape=jax.ShapeDtypeStruct((depth, V7X_SUBLANES, N), F32),
        grid=(N // tn,),
        in_specs=[
            pl.BlockSpec((V7X_SUBLANES, D), lambda j: (0, 0)),
            pl.BlockSpec((D, tn), lambda j: (0, j)),
            pl.BlockSpec((1, tn), lambda j: (0, j)),
            pl.BlockSpec((depth, 1, tn), lambda j: (0, 0, j)),
        ],
        out_specs=pl.BlockSpec((depth, V7X_SUBLANES, tn), lambda j: (0, 0, j)),
        compiler_params=_params(("arbitrary",)),
        name="ada_mod",
    )(cp, w_ada, b_ada.reshape(1, N), ada_table.reshape(depth, 1, N))
    return out[:, :B].reshape(depth, B, N_MOD, D)


def _norm_kernel(*refs, has_resid, has_router, n_experts):
    it = iter(refs)
    x_ref = next(it)
    if has_resid:
        y_ref, gt_ref = next(it), next(it)
    g_ref, sc_ref, sh_ref = next(it), next(it), next(it)
    if has_router:
        r_ref = next(it)
    if has_resid:
        xo_ref = next(it)
    h_ref = next(it)
    if has_router:
        ti_ref, tw_ref = next(it), next(it)

    x = x_ref[...]
    if has_resid:
        x = x + gt_ref[0] * y_ref[...]
        xo_ref[...] = x
    h = _rms(x, g_ref[...]) * (1.0 + sc_ref[0]) + sh_ref[0]
    h_ref[...] = h.astype(h_ref.dtype)
    if has_router:
        logits = jnp.dot(h, r_ref[...], preferred_element_type=F32, precision=lax.Precision.HIGHEST)
        lane = lax.broadcasted_iota(jnp.int32, logits.shape, 1).astype(F32)
        neg = jnp.float32(-jnp.inf)
        big = jnp.float32(V7X_LANES)
        lg = jnp.where(lane < n_experts, logits, neg)
        m1 = jnp.max(lg, axis=-1, keepdims=True)
        i1 = jnp.min(jnp.where(lg == m1, lane, big), axis=-1, keepdims=True)
        lg2 = jnp.where(lane == i1, neg, lg)
        m2 = jnp.max(lg2, axis=-1, keepdims=True)
        i2 = jnp.min(jnp.where(lg2 == m2, lane, big), axis=-1, keepdims=True)
        e = jnp.exp(m2 - m1)
        w1 = 1.0 / (1.0 + e)
        w2 = e / (1.0 + e)
        ti_ref[...] = jnp.where(lane == 0.0, i1, jnp.where(lane == 1.0, i2, 0.0)).astype(jnp.int32)
        tw_ref[...] = jnp.where(lane == 0.0, w1, jnp.where(lane == 1.0, w2, 0.0))


def _norm(x, gain, scale, shift, seq, *, resid=None, router=None, h_dtype=BF16):
    T, D = x.shape
    B = scale.shape[0]
    tm = 256
    row = pl.BlockSpec((tm, D), lambda i: (i, 0))
    per_batch = pl.BlockSpec((1, 1, D), lambda i: (i * tm // seq, 0, 0))
    args, in_specs = [x], [row]
    if resid is not None:
        args += [resid[0], resid[1].reshape(B, 1, D)]
        in_specs += [row, per_batch]
    args += [gain.reshape(1, D), scale.reshape(B, 1, D), shift.reshape(B, 1, D)]
    in_specs += [pl.BlockSpec((1, D), lambda i: (0, 0)), per_batch, per_batch]
    out_shape, out_specs = [], []
    if router is not None:
        n_experts = router.shape[1]
        rp = jnp.zeros((D, V7X_LANES), F32).at[:, :n_experts].set(router)
        args.append(rp)
        in_specs.append(pl.BlockSpec((D, V7X_LANES), lambda i: (0, 0)))
    else:
        n_experts = 0
    if resid is not None:
        out_shape.append(jax.ShapeDtypeStruct((T, D), F32))
        out_specs.append(row)
    out_shape.append(jax.ShapeDtypeStruct((T, D), h_dtype))
    out_specs.append(row)
    if router is not None:
        small = pl.BlockSpec((tm, V7X_LANES), lambda i: (i, 0))
        out_shape += [jax.ShapeDtypeStruct((T, V7X_LANES), jnp.int32), jax.ShapeDtypeStruct((T, V7X_LANES), F32)]
        out_specs += [small, small]
    return pl.pallas_call(
        functools.partial(_norm_kernel, has_resid=resid is not None, has_router=router is not None,
                          n_experts=n_experts),
        out_shape=out_shape,
        grid=(T // tm,),
        in_specs=in_specs,
        out_specs=out_specs,
        compiler_params=_params(("arbitrary",)),
        name="ada_norm",
    )(*args)


def _mm_kernel(*refs, mode, cast_w):
    it = iter(refs)
    a_ref, w_ref = next(it), next(it)
    if mode == "resid":
        r_ref, g_ref = next(it), next(it)
    o_ref = next(it)
    if cast_w:
        wscr = next(it)

        @pl.when(pl.program_id(1) == 0)
        def _():
            wscr[...] = w_ref[...].astype(BF16)

        w = wscr[...]
    else:
        w = w_ref[...]
    acc = jnp.dot(a_ref[...], w, preferred_element_type=F32)
    if mode == "plain":
        o_ref[...] = acc.astype(o_ref.dtype)
    elif mode == "sigmoid":
        o_ref[...] = jax.nn.sigmoid(acc).astype(o_ref.dtype)
    else:
        o_ref[...] = r_ref[...] + g_ref[0] * acc


def _pick_tile(n, cap):
    t = cap
    while n % t:
        t -= V7X_LANES
    return t


def _matmul(a, w, *, mode="plain", out_dtype=F32, resid=None, gate=None, seq=None, tm=1024, tn_cap=512):
    M, K = a.shape
    N = w.shape[1]
    tn = _pick_tile(N, tn_cap)
    cast_w = w.dtype != BF16
    args = [a, w]
    in_specs = [pl.BlockSpec((tm, K), lambda j, i: (i, 0)), pl.BlockSpec((K, tn), lambda j, i: (0, j))]
    if mode == "resid":
        B = gate.shape[0]
        args += [resid, gate.reshape(B, 1, N)]
        in_specs += [pl.BlockSpec((tm, tn), lambda j, i: (i, j)),
                     pl.BlockSpec((1, 1, tn), lambda j, i: (i * tm // seq, 0, j))]
    return pl.pallas_call(
        functools.partial(_mm_kernel, mode=mode, cast_w=cast_w),
        out_shape=jax.ShapeDtypeStruct((M, N), out_dtype),
        grid=(N // tn, M // tm),
        in_specs=in_specs,
        out_specs=pl.BlockSpec((tm, tn), lambda j, i: (i, j)),
        scratch_shapes=[pltpu.VMEM((K, tn), BF16)] if cast_w else [],
        compiler_params=_params(("arbitrary", "arbitrary")),
        name="matmul_" + mode,
    )(*args)


def _mla_q_kernel(cq_ref, g_ref, w_ref, cos_ref, sin_ref, o_ref, a_scr, *, heads_per_step, scale):
    @pl.when(pl.program_id(1) == 0)
    def _():
        a_scr[...] = _rms(cq_ref[...], g_ref[...]).astype(BF16)

    acc = jnp.dot(a_scr[...], w_ref[...], preferred_element_type=F32)
    cos, sin = cos_ref[...], sin_ref[...]
    L = V7X_LANES
    for h in range(heads_per_step):
        nope = acc[:, 3 * L * h:3 * L * h + L]
        rp = acc[:, 3 * L * h + L:3 * L * h + 2 * L]
        rt = acc[:, 3 * L * h + 2 * L:3 * L * h + 3 * L]
        o_ref[:, 2 * L * h:2 * L * h + L] = (nope * scale).astype(BF16)
        o_ref[:, 2 * L * h + L:2 * L * h + 2 * L] = ((rp * cos + rt * sin) * scale).astype(BF16)


def _mla_kv_kernel(ckv_ref, g_ref, w_ref, krp_ref, krr_ref, cos_ref, sin_ref, kv_ref, kr_ref, a_scr):
    @pl.when(pl.program_id(1) == 0)
    def _():
        a_scr[...] = _rms(ckv_ref[...], g_ref[...]).astype(BF16)
        kr_ref[...] = (krp_ref[...] * cos_ref[...] + krr_ref[...] * sin_ref[...]).astype(BF16)

    kv_ref[...] = jnp.dot(a_scr[...], w_ref[...], preferred_element_type=F32).astype(BF16)


def _mla_attn_kernel(q_ref, k_ref, kr_ref, v_ref, o_ref, *, tq):
    i = pl.program_id(2)
    q = q_ref[...]

    def block(j, carry, masked):
        m, l, acc = carry
        off = pl.multiple_of(j * tq, tq)
        kk = jnp.concatenate([k_ref[pl.ds(off, tq), :], kr_ref[pl.ds(off, tq), :]], axis=1)
        s = _dot_nt(q, kk)
        if masked:
            r = lax.broadcasted_iota(jnp.int32, s.shape, 0)
            c = lax.broadcasted_iota(jnp.int32, s.shape, 1)
            s = jnp.where(c <= r, s, MASK_VALUE)
        m_new = jnp.maximum(m, jnp.max(s, axis=-1, keepdims=True))
        alpha = jnp.exp(m - m_new)
        p = jnp.exp(s - m_new)
        l = alpha * l + jnp.sum(p, axis=-1, keepdims=True)
        acc = alpha * acc + jnp.dot(p.astype(BF16), v_ref[pl.ds(off, tq), :], preferred_element_type=F32)
        return m_new, l, acc

    carry = (jnp.full((tq, 1), MASK_VALUE, F32), jnp.zeros((tq, 1), F32), jnp.zeros((tq, MLA_V), F32))
    carry = lax.fori_loop(0, i, lambda j, c: block(j, c, False), carry)
    _, l, acc = block(i, carry, True)
    o_ref[...] = (acc / l).astype(o_ref.dtype)


def _mla(lat, cos_m, sin_m, q_norm, wq, kv_norm, wkv, B, S):
    T = lat.shape[0]
    L = V7X_LANES
    tm = 512
    hps = 4
    scale = MLA_QK ** -0.5
    q = pl.pallas_call(
        functools.partial(_mla_q_kernel, heads_per_step=hps, scale=scale),
        out_shape=jax.ShapeDtypeStruct((T, MLA_HEADS * MLA_QPAD), BF16),
        grid=(T // tm, MLA_HEADS // hps),
        in_specs=[
            pl.BlockSpec((tm, MLA_Q_RANK), lambda i, j: (i, 0)),
            pl.BlockSpec((1, MLA_Q_RANK), lambda i, j: (0, 0)),
            pl.BlockSpec((MLA_Q_RANK, hps * 3 * L), lambda i, j: (0, j)),
            pl.BlockSpec((tm, L), lambda i, j: (i, 0)),
            pl.BlockSpec((tm, L), lambda i, j: (i, 0)),
        ],
        out_specs=pl.BlockSpec((tm, hps * MLA_QPAD), lambda i, j: (i, j)),
        scratch_shapes=[pltpu.VMEM((tm, MLA_Q_RANK), BF16)],
        compiler_params=_params(("arbitrary", "arbitrary")),
        name="mla_q",
    )(lat, q_norm.reshape(1, -1), wq, cos_m, sin_m)

    tn = 1024
    kv_w = 2 * MLA_HEADS * MLA_NOPE
    ckv_blk = MLA_Q_RANK // MLA_KV_RANK
    krp_blk = (MLA_Q_RANK + MLA_KV_RANK) // L
    kv, kr = pl.pallas_call(
        _mla_kv_kernel,
        out_shape=[jax.ShapeDtypeStruct((T, kv_w), BF16), jax.ShapeDtypeStruct((T, L), BF16)],
        grid=(T // tm, kv_w // tn),
        in_specs=[
            pl.BlockSpec((tm, MLA_KV_RANK), lambda i, j: (i, ckv_blk)),
            pl.BlockSpec((1, MLA_KV_RANK), lambda i, j: (0, 0)),
            pl.BlockSpec((MLA_KV_RANK, tn), lambda i, j: (0, j)),
            pl.BlockSpec((tm, L), lambda i, j: (i, krp_blk)),
            pl.BlockSpec((tm, L), lambda i, j: (i, krp_blk + 1)),
            pl.BlockSpec((tm, L), lambda i, j: (i, 0)),
            pl.BlockSpec((tm, L), lambda i, j: (i, 0)),
        ],
        out_specs=[pl.BlockSpec((tm, tn), lambda i, j: (i, j)), pl.BlockSpec((tm, L), lambda i, j: (i, 0))],
        scratch_shapes=[pltpu.VMEM((tm, MLA_KV_RANK), BF16)],
        compiler_params=_params(("arbitrary", "arbitrary")),
        name="mla_kv",
    )(lat, kv_norm.reshape(1, -1), wkv, lat, lat, cos_m, sin_m)

    tq = 512
    nq = S // tq
    return pl.pallas_call(
        functools.partial(_mla_attn_kernel, tq=tq),
        out_shape=jax.ShapeDtypeStruct((T, MLA_HEADS * MLA_V), BF16),
        grid=(B, MLA_HEADS, nq),
        in_specs=[
            pl.BlockSpec((tq, MLA_QPAD), lambda b, h, i: (b * nq + i, h)),
            pl.BlockSpec((S, MLA_NOPE), lambda b, h, i: (b, h)),
            pl.BlockSpec((S, L), lambda b, h, i: (b, 0)),
            pl.BlockSpec((S, MLA_V), lambda b, h, i: (b, MLA_HEADS + h)),
        ],
        out_specs=pl.BlockSpec((tq, MLA_V), lambda b, h, i: (b * nq + i, h)),
        compiler_params=_params(("arbitrary", "arbitrary", "arbitrary")),
        name="mla_attn",
    )(q, kv, kr, kv)


def _hgrn2_kernel(q_ref, f_ref, i_ref, gate_ref, lbl_ref, ng_ref, o_ref, st_scr, q_scr, k_scr, b_scr, *, layer):
    tb = q_ref.shape[0]
    C = HG_CHUNK

    @pl.when(pl.program_id(2) == 0)
    def _():
        st_scr[...] = jnp.zeros_like(st_scr)

    lg = lbl_ref[...]
    e = jnp.exp(lg - jnp.max(lg, axis=0, keepdims=True))
    p = e / jnp.sum(e, axis=0, keepdims=True)
    lb = jnp.zeros((1, HG_D), F32)
    for l in range(1, layer + 1):
        lb = lb + p[l:l + 1, :]

    fr = f_ref[...]
    f = lb + (1.0 - lb) * jax.nn.sigmoid(fr)
    logf = jnp.log(jnp.maximum(f, HG_MIN_F))
    q_scr[...] = _silu(q_ref[...])
    k_scr[...] = (1.0 - lb) * jax.nn.sigmoid(-fr)

    hi = logf.astype(BF16)
    r1 = logf - hi.astype(F32)
    mid = r1.astype(BF16)
    lo = (r1 - mid.astype(F32)).astype(BF16)
    rr = lax.broadcasted_iota(jnp.int32, (C, C), 0)
    cc = lax.broadcasted_iota(jnp.int32, (C, C), 1)
    causal = cc <= rr
    tril = jnp.where(causal, 1.0, 0.0).astype(BF16)
    for c in range(tb // C):
        sl = slice(c * C, (c + 1) * C)
        parts = jnp.concatenate([hi[sl], mid[sl], lo[sl]], axis=1)
        cs = jnp.dot(tril, parts, preferred_element_type=F32)
        b_scr[sl, :] = cs[:, :HG_D] + cs[:, HG_D:2 * HG_D] + cs[:, 2 * HG_D:]

    st = st_scr[...]
    ng = ng_ref[...]
    for c in range(tb // C):
        r0 = c * C
        b_c = b_scr[r0:r0 + C, :]
        q_c = q_scr[r0:r0 + C, :]
        k_c = k_scr[r0:r0 + C, :]
        v_c = i_ref[r0:r0 + C, :]
        b_last = b_scr[r0 + C - 1:r0 + C, :]
        rows = []
        for blk in range(C // HG_SUB):
            s0 = r0 + blk * HG_SUB
            n = (blk + 1) * HG_SUB
            ref = b_scr[s0 + HG_SUB // 2 - 1:s0 + HG_SUB // 2, :]
            q_b = q_scr[s0:s0 + HG_SUB, :] * jnp.exp(b_scr[s0:s0 + HG_SUB, :] - ref)
            k_b = k_scr[r0:r0 + n, :] * jnp.exp(ref - b_scr[r0:r0 + n, :])
            if n < C:
                k_b = jnp.concatenate([k_b, jnp.zeros((C - n, HG_D), F32)], axis=0)
            rows.append(_dot_nt(q_b, k_b))
        scores = jnp.where(causal, jnp.concatenate(rows, axis=0), 0.0)
        o = jnp.dot(scores, v_c, preferred_element_type=F32) + _dot_nt(q_c * jnp.exp(b_c), st)
        st = st * jnp.exp(b_last) + _dot_tn(v_c, k_c * jnp.exp(b_last - b_c))
        y = _rms(o, ng) * _silu(gate_ref[r0:r0 + C, :])
        o_ref[r0:r0 + C, :] = y.astype(o_ref.dtype)
    st_scr[...] = st


def _hgrn2(hg, lb_logits, norm_g, layer, B, S):
    T = hg.shape[0]
    tb = 512
    nb = S // tb
    depth = lb_logits.shape[0]

    def part(p):
        return pl.BlockSpec((tb, HG_D), lambda b, h, j: (b * nb + j, p * HG_HEADS + h))

    return pl.pallas_call(
        functools.partial(_hgrn2_kernel, layer=layer),
        out_shape=jax.ShapeDtypeStruct((T, HG_WIDTH), BF16),
        grid=(B, HG_HEADS, nb),
        in_specs=[part(0), part(1), part(2), part(3),
                  pl.BlockSpec((depth, HG_D), lambda b, h, j: (0, h)),
                  pl.BlockSpec((1, HG_D), lambda b, h, j: (0, h))],
        out_specs=pl.BlockSpec((tb, HG_D), lambda b, h, j: (b * nb + j, h)),
        scratch_shapes=[pltpu.VMEM((HG_D, HG_D), F32), pltpu.VMEM((tb, HG_D), F32),
                        pltpu.VMEM((tb, HG_D), F32), pltpu.VMEM((tb, HG_D), F32)],
        compiler_params=_params(("arbitrary", "arbitrary", "arbitrary")),
        name="hgrn2",
    )(hg, hg, hg, hg, lb_logits, norm_g.reshape(1, -1))


def _ret_kernel(q_ref, k_ref, v_ref, gate_ref, cos_ref, sin_ref, lg_ref, ng_ref, o_ref, st_scr):
    tb = q_ref.shape[0]
    C = RET_CHUNK

    @pl.when(pl.program_id(2) == 0)
    def _():
        st_scr[...] = jnp.zeros_like(st_scr)

    cos, sin = cos_ref[...], sin_ref[...]
    half = RET_DK // 2
    xq, xk = q_ref[...], k_ref[...]
    q = xq * cos + pltpu.roll(xq, half, 1) * sin
    k = (xk * cos + pltpu.roll(xk, half, 1) * sin) * (RET_DK ** -0.5)

    lg = lg_ref[0]
    rr = lax.broadcasted_iota(jnp.int32, (C, C), 0)
    cc = lax.broadcasted_iota(jnp.int32, (C, C), 1)
    rel = (rr - cc).astype(F32)
    decay = jnp.where(rel >= 0.0, jnp.exp(lg * jnp.maximum(rel, 0.0)), 0.0)
    idx = rr.astype(F32)
    q_dec = jnp.exp(lg * (idx + 1.0))
    k_dec = jnp.exp(lg * (C - 1.0 - idx))
    chunk_dec = jnp.exp(lg * float(C))

    st = st_scr[...]
    ng = ng_ref[...]
    for c in range(tb // C):
        sl = slice(c * C, (c + 1) * C)
        q_c, k_c = q[sl], k[sl]
        v_c = v_ref[sl, :]
        s = _dot_nt(q_c.astype(BF16), k_c.astype(BF16)) * decay
        o = jnp.dot(s.astype(BF16), v_c.astype(BF16), preferred_element_type=F32)
        o = o + _dot_nt((q_c * q_dec).astype(BF16), st.astype(BF16))
        st = st * chunk_dec + jnp.dot(v_c.T.astype(BF16), (k_c * k_dec).astype(BF16),
                                      preferred_element_type=F32)
        y = _rms(o, ng) * _silu(gate_ref[sl, :])
        o_ref[sl, :] = y.astype(o_ref.dtype)
    st_scr[...] = st


def _retention(rt, cos_r, sin_r, norm_g, B, S):
    T = rt.shape[0]
    tb = 512
    nb = S // tb
    H = RET_HEADS
    v_blk0 = 2 * H * RET_DK // RET_DV
    log_gamma = jnp.log1p(-(2.0 ** (-5.0 - jnp.arange(H, dtype=F32))))
    lg = jnp.broadcast_to(log_gamma[:, None, None], (H, 1, RET_CHUNK))
    return pl.pallas_call(
        _ret_kernel,
        out_shape=jax.ShapeDtypeStruct((T, H * RET_DV), BF16),
        grid=(B, H, nb),
        in_specs=[
            pl.BlockSpec((tb, RET_DK), lambda b, h, j: (b * nb + j, h)),
            pl.BlockSpec((tb, RET_DK), lambda b, h, j: (b * nb + j, H + h)),
            pl.BlockSpec((tb, RET_DV), lambda b, h, j: (b * nb + j, v_blk0 + h)),
            pl.BlockSpec((tb, RET_DV), lambda b, h, j: (b * nb + j, v_blk0 + H + h)),
            pl.BlockSpec((tb, RET_DK), lambda b, h, j: (b * nb + j, 0)),
            pl.BlockSpec((tb, RET_DK), lambda b, h, j: (b * nb + j, 0)),
            pl.BlockSpec((1, 1, RET_CHUNK), lambda b, h, j: (h, 0, 0)),
            pl.BlockSpec((1, RET_DV), lambda b, h, j: (0, h)),
        ],
        out_specs=pl.BlockSpec((tb, RET_DV), lambda b, h, j: (b * nb + j, h)),
        scratch_shapes=[pltpu.VMEM((RET_DV, RET_DK), F32)],
        compiler_params=_params(("arbitrary", "arbitrary", "arbitrary")),
        name="retention",
    )(rt, rt, rt, rt, cos_r, sin_r, lg, norm_g.reshape(1, -1))


def _merge_kernel(ya_ref, yb_ref, yc_ref, w_ref, ga_ref, gb_ref, gc_ref, o_ref, wscr):
    @pl.when(pl.program_id(1) == 0)
    def _():
        wscr[...] = w_ref[...].astype(BF16)

    acc = ga_ref[...].astype(F32) * jnp.dot(ya_ref[...], wscr[0], preferred_element_type=F32)
    acc = acc + gb_ref[...].astype(F32) * jnp.dot(yb_ref[...], wscr[1], preferred_element_type=F32)
    acc = acc + gc_ref[...].astype(F32) * jnp.dot(yc_ref[...], wscr[2], preferred_element_type=F32)
    o_ref[...] = acc.astype(o_ref.dtype)


def _merge(ya, yb, yc, w_branch, gates):
    T, K = ya.shape
    D = w_branch.shape[2]
    tm, tn = 512, 512
    nj = D // tn
    ybs = pl.BlockSpec((tm, K), lambda j, i: (i, 0))
    return pl.pallas_call(
        _merge_kernel,
        out_shape=jax.ShapeDtypeStruct((T, D), BF16),
        grid=(nj, T // tm),
        in_specs=[ybs, ybs, ybs,
                  pl.BlockSpec((N_BRANCH, K, tn), lambda j, i: (0, 0, j)),
                  pl.BlockSpec((tm, tn), lambda j, i: (i, j)),
                  pl.BlockSpec((tm, tn), lambda j, i: (i, nj + j)),
                  pl.BlockSpec((tm, tn), lambda j, i: (i, 2 * nj + j))],
        out_specs=pl.BlockSpec((tm, tn), lambda j, i: (i, j)),
        scratch_shapes=[pltpu.VMEM((N_BRANCH, K, tn), BF16)],
        compiler_params=_params(("arbitrary", "arbitrary")),
        name="merge",
    )(ya, yb, yc, w_branch, gates, gates, gates)


def _ffn_kernel(*refs, grouped):
    it = iter(refs)
    if grouped:
        te_ref, nu_ref = next(it), next(it)
    x_ref, w1_ref, w3_ref, w2_ref = next(it), next(it), next(it), next(it)
    if grouped:
        rw_ref = next(it)
    o_ref = next(it)
    f = pl.program_id(1)
    nf = pl.num_programs(1)

    def compute():
        x = x_ref[...]
        g = jnp.dot(x, w1_ref[0], preferred_element_type=F32)
        u = jnp.dot(x, w3_ref[0], preferred_element_type=F32)
        part = jnp.dot((_silu(g) * u).astype(BF16), w2_ref[0], preferred_element_type=F32)

        @pl.when(f == 0)
        def _():
            o_ref[...] = part

        @pl.when(f > 0)
        def _():
            o_ref[...] += part

        if grouped:
            @pl.when(f == nf - 1)
            def _():
                o_ref[...] = o_ref[...] * rw_ref[...]

    if grouped:
        used = pl.program_id(0) < nu_ref[0]
        pl.when(used)(compute)

        @pl.when(jnp.logical_and(jnp.logical_not(used), f == 0))
        def _():
            o_ref[...] = jnp.zeros_like(o_ref)
    else:
        compute()


def _ffn(x, w1, w3, w2, *, tm, tf, tile_expert=None, n_used=None, row_w=None):
    M, D = x.shape
    F = w1.shape[2]
    grouped = tile_expert is not None
    if grouped:
        xs = pl.BlockSpec((tm, D), lambda i, f, te, nu: (i, 0))
        w13 = pl.BlockSpec((1, D, tf), lambda i, f, te, nu: (te[i], 0, f))
        w2s = pl.BlockSpec((1, tf, D), lambda i, f, te, nu: (te[i], f, 0))
        in_specs = [xs, w13, w13, w2s, pl.BlockSpec((tm, 1), lambda i, f, te, nu: (i, 0))]
        out_spec = pl.BlockSpec((tm, D), lambda i, f, te, nu: (i, 0))
        args = (tile_expert, n_used, x, w1, w3, w2, row_w)
        nsp = 2
    else:
        xs = pl.BlockSpec((tm, D), lambda i, f: (i, 0))
        w13 = pl.BlockSpec((1, D, tf), lambda i, f: (0, 0, f))
        w2s = pl.BlockSpec((1, tf, D), lambda i, f: (0, f, 0))
        in_specs = [xs, w13, w13, w2s]
        out_spec = pl.BlockSpec((tm, D), lambda i, f: (i, 0))
        args = (x, w1, w3, w2)
        nsp = 0
    return pl.pallas_call(
        functools.partial(_ffn_kernel, grouped=grouped),
        out_shape=jax.ShapeDtypeStruct((M, D), F32),
        grid_spec=pltpu.PrefetchScalarGridSpec(
            num_scalar_prefetch=nsp, grid=(M // tm, F // tf), in_specs=in_specs, out_specs=out_spec),
        compiler_params=_params(("arbitrary", "arbitrary")),
        name="ffn_grouped" if grouped else "ffn_dense",
    )(*args)


def _gather_kernel(src_ref, h_hbm, o_ref, buf, sem, *, rows):
    base = pl.program_id(0) * rows

    def row_copy(r):
        return pltpu.make_async_copy(h_hbm.at[pl.ds(src_ref[base + r], 1), :], buf.at[pl.ds(r, 1), :], sem)

    def issue(r, carry):
        row_copy(r).start()
        return carry

    def drain(r, carry):
        row_copy(r).wait()
        return carry

    lax.fori_loop(0, rows, issue, 0)
    lax.fori_loop(0, rows, drain, 0)
    o_ref[...] = buf[...].astype(o_ref.dtype)


def _gather_rows(h, row_src, n_rows):
    T, D = h.shape
    rows = 256
    return pl.pallas_call(
        functools.partial(_gather_kernel, rows=rows),
        out_shape=jax.ShapeDtypeStruct((n_rows, D), BF16),
        grid_spec=pltpu.PrefetchScalarGridSpec(
            num_scalar_prefetch=1, grid=(n_rows // rows,),
            in_specs=[pl.BlockSpec(memory_space=pl.ANY)],
            out_specs=pl.BlockSpec((rows, D), lambda i, src: (i, 0)),
            scratch_shapes=[pltpu.VMEM((rows, D), h.dtype), pltpu.SemaphoreType.DMA]),
        compiler_params=_params(("arbitrary",)),
        name="moe_gather",
    )(row_src, h)


def _combine_kernel(pos_ref, x_ref, gt_ref, fg_ref, ys_hbm, o_ref, buf, sem, *, rows, top_k):
    base = pl.program_id(0) * rows

    def row_copy(r, s):
        return pltpu.make_async_copy(
            ys_hbm.at[pl.ds(pos_ref[(base + r) * top_k + s], 1), :], buf.at[s, pl.ds(r, 1), :], sem)

    def issue(r, carry):
        for s in range(top_k):
            row_copy(r, s).start()
        return carry

    def drain(r, carry):
        for s in range(top_k):
            row_copy(r, s).wait()
        return carry

    lax.fori_loop(0, rows, issue, 0)
    lax.fori_loop(0, rows, drain, 0)
    y = buf[0]
    for s in range(1, top_k):
        y = y + buf[s]
    x = x_ref[...] + gt_ref[0] * y
    o_ref[...] = _rms(x, fg_ref[...])


def _combine(pos, x, gate, final_g, ys, seq, top_k):
    T, D = x.shape
    B = gate.shape[0]
    rows = 256
    return pl.pallas_call(
        functools.partial(_combine_kernel, rows=rows, top_k=top_k),
        out_shape=jax.ShapeDtypeStruct((T, D), F32),
        grid_spec=pltpu.PrefetchScalarGridSpec(
            num_scalar_prefetch=1, grid=(T // rows,),
            in_specs=[pl.BlockSpec((rows, D), lambda i, p: (i, 0)),
                      pl.BlockSpec((1, 1, D), lambda i, p: (i * rows // seq, 0, 0)),
                      pl.BlockSpec((1, D), lambda i, p: (0, 0)),
                      pl.BlockSpec(memory_space=pl.ANY)],
            out_specs=pl.BlockSpec((rows, D), lambda i, p: (i, 0)),
            scratch_shapes=[pltpu.VMEM((top_k, rows, D), F32), pltpu.SemaphoreType.DMA]),
        compiler_params=_params(("arbitrary",)),
        name="moe_combine",
    )(pos, x, gate.reshape(B, 1, D), final_g.reshape(1, D), ys)


def _route(top_i, top_w, tm, n_tiles):
    T, top_k = top_i.shape
    flat_e = top_i.reshape(-1)
    onehot = (flat_e[:, None] == jnp.arange(N_EXPERTS, dtype=jnp.int32)[None, :]).astype(jnp.int32)
    rank = jnp.sum((jnp.cumsum(onehot, axis=0) - onehot) * onehot, axis=1)
    counts = jnp.sum(onehot, axis=0)
    padded = (counts + tm - 1) // tm * tm
    ends = jnp.cumsum(padded)
    starts = ends - padded
    pos = starts[flat_e] + rank
    token = jnp.arange(T * top_k, dtype=jnp.int32) // top_k
    row_src = jnp.zeros((n_tiles * tm,), jnp.int32).at[pos].set(token)
    row_w = jnp.zeros((n_tiles * tm,), F32).at[pos].set(top_w.reshape(-1))
    tile_start = jnp.arange(n_tiles, dtype=jnp.int32) * tm
    tile_expert = jnp.minimum(jnp.sum((tile_start[:, None] >= ends[None, :]).astype(jnp.int32), axis=1),
                              N_EXPERTS - 1)
    n_used = (ends[-1] // tm).reshape(1)
    return pos.astype(jnp.int32), row_src, row_w.reshape(-1, 1), tile_expert.astype(jnp.int32), n_used.astype(jnp.int32)


def _rot_cols(w):
    half = w.shape[1] // 2
    return jnp.concatenate([-w[:, half:], w[:, :half]], axis=1)


def _pad_cols(w, width):
    return jnp.concatenate([w, jnp.zeros((w.shape[0], width - w.shape[1]), w.dtype)], axis=1)


def _latent_weight(w_in_l):
    n_lat = MLA_Q_RANK + MLA_KV_RANK
    kr = w_in_l[:, n_lat:n_lat + MLA_ROPE]
    return jnp.concatenate([w_in_l[:, :n_lat], _pad_cols(kr, V7X_LANES), _pad_cols(_rot_cols(kr), V7X_LANES)],
                           axis=1).astype(BF16)


def _q_up_weight(w):
    r = w.shape[0]
    w = w.reshape(r, MLA_HEADS, MLA_QK)
    nope, rp = w[:, :, :MLA_NOPE], w[:, :, MLA_NOPE:]
    half = MLA_ROPE // 2
    rot = jnp.concatenate([-rp[:, :, half:], rp[:, :, :half]], axis=2)
    z = jnp.zeros((r, MLA_HEADS, V7X_LANES - MLA_ROPE), w.dtype)
    return jnp.concatenate([nope, rp, z, rot, z], axis=2).reshape(r, -1).astype(BF16)


def _kv_up_weight(w):
    r = w.shape[0]
    w = w.reshape(r, MLA_HEADS, MLA_NOPE + MLA_V)
    return jnp.concatenate([w[:, :, :MLA_NOPE].reshape(r, -1), w[:, :, MLA_NOPE:].reshape(r, -1)],
                           axis=1).astype(BF16)


def _rope_tables(positions):
    pos = positions.reshape(-1).astype(F32)[:, None]
    inv_m = ROPE_BASE ** (-jnp.arange(0, MLA_ROPE, 2, dtype=F32) / MLA_ROPE)
    am = pos * inv_m
    zm = jnp.zeros((pos.shape[0], V7X_LANES - MLA_ROPE), F32)
    cos_m = jnp.concatenate([jnp.cos(am), jnp.cos(am), zm], axis=1)
    sin_m = jnp.concatenate([jnp.sin(am), jnp.sin(am), zm], axis=1)
    inv_r = ROPE_BASE ** (-jnp.arange(0, RET_DK, 2, dtype=F32) / RET_DK)
    ar = pos * inv_r
    cos_r = jnp.concatenate([jnp.cos(ar), jnp.cos(ar)], axis=1)
    sin_r = jnp.concatenate([-jnp.sin(ar), jnp.sin(ar)], axis=1)
    return cos_m, sin_m, cos_r, sin_r


def kernel(x, c, positions, w_ada, b_ada, ada_table, norm_g, w_in, mla_q_norm, mla_w_q_up, mla_kv_norm,
           mla_w_kv_up, hg_lb_logits, hg_norm, ret_norm, w_branch, w_out, ffn_w1, ffn_w3, ffn_w2,
           moe_router, moe_w1, moe_w3, moe_w2, final_norm):
    B, S, D = x.shape
    T = B * S
    depth = norm_g.shape[0]
    top_k = 2
    assert depth % 2 == 0, "the final norm is fused into the expert combine of the last (odd) layer"
    assert S % 512 == 0 and D % 512 == 0

    mod = _ada(c, w_ada, b_ada, ada_table)
    cos_m, sin_m, cos_r, sin_r = _rope_tables(positions)

    n_lat = MLA_Q_RANK + MLA_KV_RANK + MLA_ROPE
    n_hg = 4 * HG_WIDTH
    n_ret = 2 * RET_HEADS * RET_DK + 2 * RET_HEADS * RET_DV

    xf = x.reshape(T, D)
    pending = None
    out = None
    for l in range(depth):
        shift1, scale1, gate1, shift2, scale2, gate2 = (mod[l, :, j] for j in range(N_MOD))
        if pending is None:
            (h,) = _norm(xf, norm_g[l, 0], scale1, shift1, S)
        else:
            xf, h = _norm(xf, norm_g[l, 0], scale1, shift1, S, resid=pending)
            pending = None

        w_rest = w_in[l][:, n_lat:].astype(BF16)
        lat = _matmul(h, _latent_weight(w_in[l]), tn_cap=256)
        hg = _matmul(h, w_rest[:, :n_hg])
        rt = _matmul(h, w_rest[:, n_hg:n_hg + n_ret])
        gates = _matmul(h, w_rest[:, n_hg + n_ret:], mode="sigmoid", out_dtype=BF16)

        y_mla = _mla(lat, cos_m, sin_m, mla_q_norm[l], _q_up_weight(mla_w_q_up[l]),
                     mla_kv_norm[l], _kv_up_weight(mla_w_kv_up[l]), B, S)
        y_hg = _hgrn2(hg, hg_lb_logits, hg_norm[l], l, B, S)
        y_ret = _retention(rt, cos_r, sin_r, ret_norm[l], B, S)

        merged = _merge(y_mla, y_hg, y_ret, w_branch[l], gates)
        xf = _matmul(merged, w_out[l], mode="resid", resid=xf, gate=gate1, seq=S)

        if l % 2 == 0:
            (h2,) = _norm(xf, norm_g[l, 1], scale2, shift2, S)
            e = l // 2
            y = _ffn(h2, ffn_w1[e:e + 1].astype(BF16), ffn_w3[e:e + 1].astype(BF16),
                     ffn_w2[e:e + 1].astype(BF16), tm=512, tf=256)
            pending = (y, gate2)
        else:
            e = l // 2
            h2, ti, tw = _norm(xf, norm_g[l, 1], scale2, shift2, S, router=moe_router[e], h_dtype=F32)
            tm = 512
            n_tiles = T * top_k // tm + N_EXPERTS
            pos, row_src, row_w, tile_expert, n_used = _route(ti[:, :top_k], tw[:, :top_k], tm, n_tiles)
            xs = _gather_rows(h2, row_src, n_tiles * tm)
            ys = _ffn(xs, moe_w1[e].astype(BF16), moe_w3[e].astype(BF16), moe_w2[e].astype(BF16),
                      tm=tm, tf=256, tile_expert=tile_expert, n_used=n_used, row_w=row_w)
            assert l == depth - 1
            out = _combine(pos, xf, gate2, final_norm, ys, S, top_k)
    return out.reshape(B, S, D)
```

```python
import functools
import math

import jax
import jax.numpy as jnp
from jax import lax
from jax.experimental import pallas as pl
from jax.experimental.pallas import tpu as pltpu

F32 = jnp.float32
BF16 = jnp.bfloat16

V7X_LANES = 128
V7X_SUBLANES = 8
V7X_VMEM_BYTES = 64 * 1024 * 1024
V7X_VMEM_LIMIT = V7X_VMEM_BYTES - 8 * 1024 * 1024

EPS = 1e-6
ROPE_BASE = 10000.0
MASK_VALUE = -1e30

MLA_HEADS = 16
MLA_NOPE = 128
MLA_ROPE = 64
MLA_V = 128
MLA_Q_RANK = 1024
MLA_KV_RANK = 512
MLA_QK = MLA_NOPE + MLA_ROPE
MLA_QPAD = 2 * V7X_LANES

HG_HEADS = 16
HG_D = 128
HG_WIDTH = HG_HEADS * HG_D
HG_CHUNK = 64
HG_SUB = 8
HG_MIN_F = 1e-6

RET_HEADS = 8
RET_DK = 128
RET_DV = 256
RET_CHUNK = 128

N_BRANCH = 3
N_EXPERTS = 8
N_MOD = 6


def _params(semantics):
    return pltpu.CompilerParams(dimension_semantics=semantics, vmem_limit_bytes=V7X_VMEM_LIMIT)


def _rms(x, gain):
    return x * lax.rsqrt(jnp.mean(x * x, axis=-1, keepdims=True) + EPS) * gain


def _silu(x):
    return x * jax.nn.sigmoid(x)


def _dot_nt(a, b):
    return lax.dot_general(a, b, (((1,), (1,)), ((), ())), preferred_element_type=F32)


def _dot_tn(a, b):
    return jnp.dot(a.T, b, preferred_element_type=F32)


def _ada_kernel(c_ref, w_ref, b_ref, tab_ref, o_ref, acc_ref):
    k = pl.program_id(0)

    @pl.when(k == 0)
    def _():
        acc_ref[...] = jnp.zeros_like(acc_ref)

    s = _silu(c_ref[...]).astype(BF16)
    acc_ref[...] += jnp.dot(s, w_ref[...].astype(BF16), preferred_element_type=F32)

    @pl.when(k == pl.num_programs(0) - 1)
    def _():
        acc = acc_ref[...] + b_ref[...]
        for l in range(o_ref.shape[0]):
            o_ref[l] = acc + tab_ref[l]


def _ada(c, w_ada, b_ada, ada_table):
    B, D = c.shape
    depth = ada_table.shape[0]
    N = w_ada.shape[1]
    tk = V7X_LANES
    cp = jnp.zeros((V7X_SUBLANES, D), F32).at[:B].set(c)
    out = pl.pallas_call(
        _ada_kernel,
        out_shape=jax.ShapeDtypeStruct((depth, V7X_SUBLANES, N), F32),
        grid=(D // tk,),
        in_specs=[
            pl.BlockSpec((V7X_SUBLANES, tk), lambda k: (0, k)),
            pl.BlockSpec((tk, N), lambda k: (k, 0)),
            pl.BlockSpec((1, N), lambda k: (0, 0)),
            pl.BlockSpec((depth, 1, N), lambda k: (0, 0, 0)),
        ],
        out_specs=pl.BlockSpec((depth, V7X_SUBLANES, N), lambda k: (0, 0, 0)),
        scratch_shapes=[pltpu.VMEM((V7X_SUBLANES, N), F32)],
        compiler_params=_params(("arbitrary",)),
        name="ada_mod",
    )(cp, w_ada, b_ada.reshape(1, N), ada_table.reshape(depth, 1, N))
    return out[:, :B].reshape(depth, B, N_MOD, D)


def _norm_kernel(*refs, has_resid, has_router, n_experts):
    it = iter(refs)
    x_ref = next(it)
    if has_resid:
        y_ref, gt_ref = next(it), next(it)
    g_ref, sc_ref, sh_ref = next(it), next(it), next(it)
    if has_router:
        r_ref = next(it)
    if has_resid:
        xo_ref = next(it)
    h_ref = next(it)
    if has_router:
        ti_ref, tw_ref = next(it), next(it)

    x = x_ref[...]
    if has_resid:
        x = x + gt_ref[0] * y_ref[...]
        xo_ref[...] = x
    h = _rms(x, g_ref[...]) * (1.0 + sc_ref[0]) + sh_ref[0]
    h_ref[...] = h.astype(h_ref.dtype)
    if has_router:
        logits = jnp.dot(h, r_ref[...], preferred_element_type=F32, precision=lax.Precision.HIGHEST)
        lane = lax.broadcasted_iota(jnp.int32, logits.shape, 1).astype(F32)
        neg = jnp.float32(-jnp.inf)
        big = jnp.float32(V7X_LANES)
        lg = jnp.where(lane < n_experts, logits, neg)
        m1 = jnp.max(lg, axis=-1, keepdims=True)
        i1 = jnp.min(jnp.where(lg == m1, lane, big), axis=-1, keepdims=True)
        lg2 = jnp.where(lane == i1, neg, lg)
        m2 = jnp.max(lg2, axis=-1, keepdims=True)
        i2 = jnp.min(jnp.where(lg2 == m2, lane, big), axis=-1, keepdims=True)
        e = jnp.exp(m2 - m1)
        w1 = 1.0 / (1.0 + e)
        w2 = e / (1.0 + e)
        ti_ref[...] = jnp.where(lane == 0.0, i1, jnp.where(lane == 1.0, i2, 0.0)).astype(jnp.int32)
        tw_ref[...] = jnp.where(lane == 0.0, w1, jnp.where(lane == 1.0, w2, 0.0))


def _norm(x, gain, scale, shift, seq, *, resid=None, router=None, h_dtype=BF16):
    T, D = x.shape
    B = scale.shape[0]
    tm = 256
    row = pl.BlockSpec((tm, D), lambda i: (i, 0))
    per_batch = pl.BlockSpec((1, 1, D), lambda i: (i * tm // seq, 0, 0))
    args, in_specs = [x], [row]
    if resid is not None:
        args += [resid[0], resid[1].reshape(B, 1, D)]
        in_specs += [row, per_batch]
    args += [gain.reshape(1, D), scale.reshape(B, 1, D), shift.reshape(B, 1, D)]
    in_specs += [pl.BlockSpec((1, D), lambda i: (0, 0)), per_batch, per_batch]
    out_shape, out_specs = [], []
    if router is not None:
        n_experts = router.shape[1]
        rp = jnp.zeros((D, V7X_LANES), F32).at[:, :n_experts].set(router)
        args.append(rp)
        in_specs.append(pl.BlockSpec((D, V7X_LANES), lambda i: (0, 0)))
    else:
        n_experts = 0
    if resid is not None:
        out_shape.append(jax.ShapeDtypeStruct((T, D), F32))
        out_specs.append(row)
    out_shape.append(jax.ShapeDtypeStruct((T, D), h_dtype))
    out_specs.append(row)
    if router is not None:
        small = pl.BlockSpec((tm, V7X_LANES), lambda i: (i, 0))
        out_shape += [jax.ShapeDtypeStruct((T, V7X_LANES), jnp.int32), jax.ShapeDtypeStruct((T, V7X_LANES), F32)]
        out_specs += [small, small]
    return pl.pallas_call(
        functools.partial(_norm_kernel, has_resid=resid is not None, has_router=router is not None,
                          n_experts=n_experts),
        out_shape=out_shape,
        grid=(T // tm,),
        in_specs=in_specs,
        out_specs=out_specs,
        compiler_params=_params(("arbitrary",)),
        name="ada_norm",
    )(*args)


_W_CAST_ROWS = 512


def _mm_kernel(*refs, mode, lane_shift):
    it = iter(refs)
    a_ref, w_ref = next(it), next(it)
    if lane_shift:
        wn_ref = next(it)
    if mode == "resid":
        r_ref, g_ref = next(it), next(it)
    o_ref, wscr = next(it), next(it)
    K, tn = wscr.shape

    @pl.when(pl.program_id(1) == 0)
    def _():
        for r0 in range(0, K, _W_CAST_ROWS):
            rows = slice(r0, r0 + _W_CAST_ROWS)
            w = w_ref[0, rows, :]
            if lane_shift:
                wide = jnp.concatenate([w, wn_ref[0, rows, :]], axis=1)
                w = pltpu.roll(wide, wide.shape[1] - lane_shift, 1)[:, :tn]
            wscr[rows, :] = w.astype(BF16)

    acc = jnp.dot(a_ref[...], wscr[...], preferred_element_type=F32)
    if mode == "plain":
        o_ref[...] = acc.astype(o_ref.dtype)
    elif mode == "sigmoid":
        o_ref[...] = jax.nn.sigmoid(acc).astype(o_ref.dtype)
    else:
        o_ref[...] = r_ref[...] + g_ref[0] * acc


def _matmul(a, w, layer, col0, n, *, mode="plain", out_dtype=F32, resid=None, gate=None, seq=None, tm=1024,
            tn=512):
    M, K = a.shape
    L = V7X_LANES
    lane_shift = col0 % L
    c0 = col0 - lane_shift
    assert n % tn == 0 and c0 % tn == 0 and K % _W_CAST_ROWS == 0
    args = [a, w]
    in_specs = [pl.BlockSpec((tm, K), lambda j, i: (i, 0)),
                pl.BlockSpec((1, K, tn), lambda j, i: (layer, 0, c0 // tn + j))]
    if lane_shift:
        args.append(w)
        in_specs.append(pl.BlockSpec((1, K, L), lambda j, i: (layer, 0, (c0 + (j + 1) * tn) // L)))
    if mode == "resid":
        B = gate.shape[0]
        args += [resid, gate.reshape(B, 1, n)]
        in_specs += [pl.BlockSpec((tm, tn), lambda j, i: (i, j)),
                     pl.BlockSpec((1, 1, tn), lambda j, i: (i * tm // seq, 0, j))]
    return pl.pallas_call(
        functools.partial(_mm_kernel, mode=mode, lane_shift=lane_shift),
        out_shape=jax.ShapeDtypeStruct((M, n), out_dtype),
        grid=(n // tn, M // tm),
        in_specs=in_specs,
        out_specs=pl.BlockSpec((tm, tn), lambda j, i: (i, j)),
        scratch_shapes=[pltpu.VMEM((K, tn), BF16)],
        compiler_params=_params(("arbitrary", "arbitrary")),
        name="matmul_" + mode,
    )(*args)


def _mla_q_kernel(cq_ref, g_ref, w_ref, cos_ref, sin_ref, o_ref, a_scr, *, heads_per_step, scale):
    @pl.when(pl.program_id(1) == 0)
    def _():
        a_scr[...] = _rms(cq_ref[...], g_ref[...]).astype(BF16)

    acc = jnp.dot(a_scr[...], w_ref[...], preferred_element_type=F32)
    cos, sin = cos_ref[...], sin_ref[...]
    L = V7X_LANES
    for h in range(heads_per_step):
        nope = acc[:, 3 * L * h:3 * L * h + L]
        rp = acc[:, 3 * L * h + L:3 * L * h + 2 * L]
        rt = acc[:, 3 * L * h + 2 * L:3 * L * h + 3 * L]
        o_ref[:, 2 * L * h:2 * L * h + L] = (nope * scale).astype(BF16)
        o_ref[:, 2 * L * h + L:2 * L * h + 2 * L] = ((rp * cos + rt * sin) * scale).astype(BF16)


def _mla_kv_kernel(ckv_ref, g_ref, w_ref, kraw_ref, cos_ref, sin_ref, kv_ref, kr_ref, a_scr):
    @pl.when(pl.program_id(1) == 0)
    def _():
        a_scr[...] = _rms(ckv_ref[...], g_ref[...]).astype(BF16)
        x = kraw_ref[...]
        half = MLA_ROPE // 2
        lane = lax.broadcasted_iota(jnp.int32, x.shape, 1)
        swapped = jnp.where(lane < half, -pltpu.roll(x, V7X_LANES - half, 1), pltpu.roll(x, half, 1))
        roped = x * cos_ref[...] + swapped * sin_ref[...]
        kr_ref[...] = jnp.where(lane < MLA_ROPE, roped, 0.0).astype(BF16)

    kv_ref[...] = jnp.dot(a_scr[...], w_ref[...], preferred_element_type=F32).astype(BF16)


def _mla_attn_kernel(q_ref, k_ref, kr_ref, v_ref, o_ref, *, tq):
    i = pl.program_id(2)
    q = q_ref[...]

    def block(j, carry, masked):
        m, l, acc = carry
        off = pl.multiple_of(j * tq, tq)
        kk = jnp.concatenate([k_ref[pl.ds(off, tq), :], kr_ref[pl.ds(off, tq), :]], axis=1)
        s = _dot_nt(q, kk)
        if masked:
            r = lax.broadcasted_iota(jnp.int32, s.shape, 0)
            c = lax.broadcasted_iota(jnp.int32, s.shape, 1)
            s = jnp.where(c <= r, s, MASK_VALUE)
        m_new = jnp.maximum(m, jnp.max(s, axis=-1, keepdims=True))
        alpha = jnp.exp(m - m_new)
        p = jnp.exp(s - m_new)
        l = alpha * l + jnp.sum(p, axis=-1, keepdims=True)
        acc = alpha * acc + jnp.dot(p.astype(BF16), v_ref[pl.ds(off, tq), :], preferred_element_type=F32)
        return m_new, l, acc

    carry = (jnp.full((tq, 1), MASK_VALUE, F32), jnp.zeros((tq, 1), F32), jnp.zeros((tq, MLA_V), F32))
    carry = lax.fori_loop(0, i, lambda j, c: block(j, c, False), carry)
    _, l, acc = block(i, carry, True)
    o_ref[...] = (acc / l).astype(o_ref.dtype)


def _mla(lat, kr_raw, cos_m, sin_m, q_norm, wq, kv_norm, wkv, B, S):
    T = lat.shape[0]
    L = V7X_LANES
    tm = 512
    hps = 4
    scale = MLA_QK ** -0.5
    q = pl.pallas_call(
        functools.partial(_mla_q_kernel, heads_per_step=hps, scale=scale),
        out_shape=jax.ShapeDtypeStruct((T, MLA_HEADS * MLA_QPAD), BF16),
        grid=(T // tm, MLA_HEADS // hps),
        in_specs=[
            pl.BlockSpec((tm, MLA_Q_RANK), lambda i, j: (i, 0)),
            pl.BlockSpec((1, MLA_Q_RANK), lambda i, j: (0, 0)),
            pl.BlockSpec((MLA_Q_RANK, hps * 3 * L), lambda i, j: (0, j)),
            pl.BlockSpec((tm, L), lambda i, j: (i, 0)),
            pl.BlockSpec((tm, L), lambda i, j: (i, 0)),
        ],
        out_specs=pl.BlockSpec((tm, hps * MLA_QPAD), lambda i, j: (i, j)),
        scratch_shapes=[pltpu.VMEM((tm, MLA_Q_RANK), BF16)],
        compiler_params=_params(("arbitrary", "arbitrary")),
        name="mla_q",
    )(lat, q_norm.reshape(1, -1), wq, cos_m, sin_m)

    tn = 1024
    kv_w = 2 * MLA_HEADS * MLA_NOPE
    ckv_blk = MLA_Q_RANK // MLA_KV_RANK
    kv, kr = pl.pallas_call(
        _mla_kv_kernel,
        out_shape=[jax.ShapeDtypeStruct((T, kv_w), BF16), jax.ShapeDtypeStruct((T, L), BF16)],
        grid=(T // tm, kv_w // tn),
        in_specs=[
            pl.BlockSpec((tm, MLA_KV_RANK), lambda i, j: (i, ckv_blk)),
            pl.BlockSpec((1, MLA_KV_RANK), lambda i, j: (0, 0)),
            pl.BlockSpec((MLA_KV_RANK, tn), lambda i, j: (0, j)),
            pl.BlockSpec((tm, L), lambda i, j: (i, 0)),
            pl.BlockSpec((tm, L), lambda i, j: (i, 0)),
            pl.BlockSpec((tm, L), lambda i, j: (i, 0)),
        ],
        out_specs=[pl.BlockSpec((tm, tn), lambda i, j: (i, j)), pl.BlockSpec((tm, L), lambda i, j: (i, 0))],
        scratch_shapes=[pltpu.VMEM((tm, MLA_KV_RANK), BF16)],
        compiler_params=_params(("arbitrary", "arbitrary")),
        name="mla_kv",
    )(lat, kv_norm.reshape(1, -1), wkv, kr_raw, cos_m, sin_m)

    tq = 512
    nq = S // tq
    return pl.pallas_call(
        functools.partial(_mla_attn_kernel, tq=tq),
        out_shape=jax.ShapeDtypeStruct((T, MLA_HEADS * MLA_V), BF16),
        grid=(B, MLA_HEADS, nq),
        in_specs=[
            pl.BlockSpec((tq, MLA_QPAD), lambda b, h, i: (b * nq + i, h)),
            pl.BlockSpec((S, MLA_NOPE), lambda b, h, i: (b, h)),
            pl.BlockSpec((S, L), lambda b, h, i: (b, 0)),
            pl.BlockSpec((S, MLA_V), lambda b, h, i: (b, MLA_HEADS + h)),
        ],
        out_specs=pl.BlockSpec((tq, MLA_V), lambda b, h, i: (b * nq + i, h)),
        compiler_params=_params(("arbitrary", "arbitrary", "arbitrary")),
        name="mla_attn",
    )(q, kv, kr, kv)


def _hgrn2_kernel(q_ref, f_ref, i_ref, gate_ref, lbl_ref, ng_ref, o_ref, st_scr, q_scr, k_scr, b_scr, *, layer):
    tb = q_ref.shape[0]
    C = HG_CHUNK

    @pl.when(pl.program_id(2) == 0)
    def _():
        st_scr[...] = jnp.zeros_like(st_scr)

    lg = lbl_ref[...]
    e = jnp.exp(lg - jnp.max(lg, axis=0, keepdims=True))
    p = e / jnp.sum(e, axis=0, keepdims=True)
    lb = jnp.zeros((1, HG_D), F32)
    for l in range(1, layer + 1):
        lb = lb + p[l:l + 1, :]

    fr = f_ref[...]
    f = lb + (1.0 - lb) * jax.nn.sigmoid(fr)
    logf = jnp.log(jnp.maximum(f, HG_MIN_F))
    q_scr[...] = _silu(q_ref[...])
    k_scr[...] = (1.0 - lb) * jax.nn.sigmoid(-fr)

    hi = logf.astype(BF16)
    r1 = logf - hi.astype(F32)
    mid = r1.astype(BF16)
    lo = (r1 - mid.astype(F32)).astype(BF16)
    rr = lax.broadcasted_iota(jnp.int32, (C, C), 0)
    cc = lax.broadcasted_iota(jnp.int32, (C, C), 1)
    causal = cc <= rr
    tril = jnp.where(causal, 1.0, 0.0).astype(BF16)
    for c in range(tb // C):
        sl = slice(c * C, (c + 1) * C)
        parts = jnp.concatenate([hi[sl], mid[sl], lo[sl]], axis=1)
        cs = jnp.dot(tril, parts, preferred_element_type=F32)
        b_scr[sl, :] = cs[:, :HG_D] + cs[:, HG_D:2 * HG_D] + cs[:, 2 * HG_D:]

    st = st_scr[...]
    ng = ng_ref[...]
    for c in range(tb // C):
        r0 = c * C
        b_c = b_scr[r0:r0 + C, :]
        q_c = q_scr[r0:r0 + C, :]
        k_c = k_scr[r0:r0 + C, :]
        v_c = i_ref[r0:r0 + C, :]
        b_last = b_scr[r0 + C - 1:r0 + C, :]
        rows = []
        for blk in range(C // HG_SUB):
            s0 = r0 + blk * HG_SUB
            n = (blk + 1) * HG_SUB
            ref = b_scr[s0 + HG_SUB // 2 - 1:s0 + HG_SUB // 2, :]
            q_b = q_scr[s0:s0 + HG_SUB, :] * jnp.exp(b_scr[s0:s0 + HG_SUB, :] - ref)
            k_b = k_scr[r0:r0 + n, :] * jnp.exp(ref - b_scr[r0:r0 + n, :])
            if n < C:
                k_b = jnp.concatenate([k_b, jnp.zeros((C - n, HG_D), F32)], axis=0)
            rows.append(_dot_nt(q_b, k_b))
        scores = jnp.where(causal, jnp.concatenate(rows, axis=0), 0.0)
        o = jnp.dot(scores, v_c, preferred_element_type=F32) + _dot_nt(q_c * jnp.exp(b_c), st)
        st = st * jnp.exp(b_last) + _dot_tn(v_c, k_c * jnp.exp(b_last - b_c))
        y = _rms(o, ng) * _silu(gate_ref[r0:r0 + C, :])
        o_ref[r0:r0 + C, :] = y.astype(o_ref.dtype)
    st_scr[...] = st


def _hgrn2(hg, lb_logits, norm_g, layer, B, S):
    T = hg.shape[0]
    tb = 512
    nb = S // tb
    depth = lb_logits.shape[0]

    def part(p):
        return pl.BlockSpec((tb, HG_D), lambda b, h, j: (b * nb + j, p * HG_HEADS + h))

    return pl.pallas_call(
        functools.partial(_hgrn2_kernel, layer=layer),
        out_shape=jax.ShapeDtypeStruct((T, HG_WIDTH), BF16),
        grid=(B, HG_HEADS, nb),
        in_specs=[part(0), part(1), part(2), part(3),
                  pl.BlockSpec((depth, HG_D), lambda b, h, j: (0, h)),
                  pl.BlockSpec((1, HG_D), lambda b, h, j: (0, h))],
        out_specs=pl.BlockSpec((tb, HG_D), lambda b, h, j: (b * nb + j, h)),
        scratch_shapes=[pltpu.VMEM((HG_D, HG_D), F32), pltpu.VMEM((tb, HG_D), F32),
                        pltpu.VMEM((tb, HG_D), F32), pltpu.VMEM((tb, HG_D), F32)],
        compiler_params=_params(("arbitrary", "arbitrary", "arbitrary")),
        name="hgrn2",
    )(hg, hg, hg, hg, lb_logits, norm_g.reshape(1, -1))


def _ret_kernel(q_ref, k_ref, v_ref, gate_ref, cos_ref, sin_ref, lg_ref, ng_ref, o_ref, st_scr):
    tb = q_ref.shape[0]
    C = RET_CHUNK

    @pl.when(pl.program_id(2) == 0)
    def _():
        st_scr[...] = jnp.zeros_like(st_scr)

    cos, sin = cos_ref[...], sin_ref[...]
    half = RET_DK // 2
    xq, xk = q_ref[...], k_ref[...]
    q = xq * cos + pltpu.roll(xq, half, 1) * sin
    k = (xk * cos + pltpu.roll(xk, half, 1) * sin) * (RET_DK ** -0.5)

    lg = lg_ref[0]
    rr = lax.broadcasted_iota(jnp.int32, (C, C), 0)
    cc = lax.broadcasted_iota(jnp.int32, (C, C), 1)
    rel = (rr - cc).astype(F32)
    decay = jnp.where(rel >= 0.0, jnp.exp(lg * jnp.maximum(rel, 0.0)), 0.0)
    idx = rr.astype(F32)
    q_dec = jnp.exp(lg * (idx + 1.0))
    k_dec = jnp.exp(lg * (C - 1.0 - idx))
    chunk_dec = jnp.exp(lg * float(C))

    st = st_scr[...]
    ng = ng_ref[...]
    for c in range(tb // C):
        sl = slice(c * C, (c + 1) * C)
        q_c, k_c = q[sl], k[sl]
        v_c = v_ref[sl, :]
        s = _dot_nt(q_c.astype(BF16), k_c.astype(BF16)) * decay
        o = jnp.dot(s.astype(BF16), v_c.astype(BF16), preferred_element_type=F32)
        o = o + _dot_nt((q_c * q_dec).astype(BF16), st.astype(BF16))
        st = st * chunk_dec + jnp.dot(v_c.T.astype(BF16), (k_c * k_dec).astype(BF16),
                                      preferred_element_type=F32)
        y = _rms(o, ng) * _silu(gate_ref[sl, :])
        o_ref[sl, :] = y.astype(o_ref.dtype)
    st_scr[...] = st


def _retention(rt, cos_r, sin_r, norm_g, B, S):
    T = rt.shape[0]
    tb = 512
    nb = S // tb
    H = RET_HEADS
    v_blk0 = 2 * H * RET_DK // RET_DV
    log_gamma = jnp.log1p(-(2.0 ** (-5.0 - jnp.arange(H, dtype=F32))))
    lg = jnp.broadcast_to(log_gamma[:, None, None], (H, 1, RET_CHUNK))
    return pl.pallas_call(
        _ret_kernel,
        out_shape=jax.ShapeDtypeStruct((T, H * RET_DV), BF16),
        grid=(B, H, nb),
        in_specs=[
            pl.BlockSpec((tb, RET_DK), lambda b, h, j: (b * nb + j, h)),
            pl.BlockSpec((tb, RET_DK), lambda b, h, j: (b * nb + j, H + h)),
            pl.BlockSpec((tb, RET_DV), lambda b, h, j: (b * nb + j, v_blk0 + h)),
            pl.BlockSpec((tb, RET_DV), lambda b, h, j: (b * nb + j, v_blk0 + H + h)),
            pl.BlockSpec((tb, RET_DK), lambda b, h, j: (b * nb + j, 0)),
            pl.BlockSpec((tb, RET_DK), lambda b, h, j: (b * nb + j, 0)),
            pl.BlockSpec((1, 1, RET_CHUNK), lambda b, h, j: (h, 0, 0)),
            pl.BlockSpec((1, RET_DV), lambda b, h, j: (0, h)),
        ],
        out_specs=pl.BlockSpec((tb, RET_DV), lambda b, h, j: (b * nb + j, h)),
        scratch_shapes=[pltpu.VMEM((RET_DV, RET_DK), F32)],
        compiler_params=_params(("arbitrary", "arbitrary", "arbitrary")),
        name="retention",
    )(rt, rt, rt, rt, cos_r, sin_r, lg, norm_g.reshape(1, -1))


def _merge_kernel(ya_ref, yb_ref, yc_ref, w_ref, ga_ref, gb_ref, gc_ref, o_ref, wscr):
    @pl.when(pl.program_id(1) == 0)
    def _():
        for br in range(N_BRANCH):
            for r0 in range(0, wscr.shape[1], _W_CAST_ROWS):
                wscr[br, r0:r0 + _W_CAST_ROWS, :] = w_ref[0, br, r0:r0 + _W_CAST_ROWS, :].astype(BF16)

    acc = ga_ref[...].astype(F32) * jnp.dot(ya_ref[...], wscr[0], preferred_element_type=F32)
    acc = acc + gb_ref[...].astype(F32) * jnp.dot(yb_ref[...], wscr[1], preferred_element_type=F32)
    acc = acc + gc_ref[...].astype(F32) * jnp.dot(yc_ref[...], wscr[2], preferred_element_type=F32)
    o_ref[...] = acc.astype(o_ref.dtype)


def _merge(ya, yb, yc, w_branch, layer, gates):
    T, K = ya.shape
    D = w_branch.shape[3]
    tm, tn = 512, 512
    nj = D // tn
    ybs = pl.BlockSpec((tm, K), lambda j, i: (i, 0))
    return pl.pallas_call(
        _merge_kernel,
        out_shape=jax.ShapeDtypeStruct((T, D), BF16),
        grid=(nj, T // tm),
        in_specs=[ybs, ybs, ybs,
                  pl.BlockSpec((1, N_BRANCH, K, tn), lambda j, i: (layer, 0, 0, j)),
                  pl.BlockSpec((tm, tn), lambda j, i: (i, j)),
                  pl.BlockSpec((tm, tn), lambda j, i: (i, nj + j)),
                  pl.BlockSpec((tm, tn), lambda j, i: (i, 2 * nj + j))],
        out_specs=pl.BlockSpec((tm, tn), lambda j, i: (i, j)),
        scratch_shapes=[pltpu.VMEM((N_BRANCH, K, tn), BF16)],
        compiler_params=_params(("arbitrary", "arbitrary")),
        name="merge",
    )(ya, yb, yc, w_branch, gates, gates, gates)


def _ffn_kernel(*refs, grouped):
    it = iter(refs)
    if grouped:
        te_ref, nu_ref = next(it), next(it)
    x_ref, w1_ref, w3_ref, w2_ref = next(it), next(it), next(it), next(it)
    if grouped:
        rw_ref = next(it)
    o_ref = next(it)
    f = pl.program_id(1)
    nf = pl.num_programs(1)

    @pl.when(f == 0)
    def _():
        o_ref[...] = jnp.zeros_like(o_ref)

    def compute():
        x = x_ref[...]
        g = jnp.dot(x, w1_ref[0], preferred_element_type=F32)
        u = jnp.dot(x, w3_ref[0], preferred_element_type=F32)
        o_ref[...] += jnp.dot((_silu(g) * u).astype(BF16), w2_ref[0], preferred_element_type=F32)

        if grouped:
            @pl.when(f == nf - 1)
            def _():
                o_ref[...] = o_ref[...] * rw_ref[...]

    if grouped:
        pl.when(pl.program_id(0) < nu_ref[0])(compute)
    else:
        compute()


def _ffn(x, w1, w3, w2, *, tm, tf, tile_expert=None, n_used=None, row_w=None):
    M, D = x.shape
    F = w1.shape[2]
    grouped = tile_expert is not None
    if grouped:
        xs = pl.BlockSpec((tm, D), lambda i, f, te, nu: (i, 0))
        w13 = pl.BlockSpec((1, D, tf), lambda i, f, te, nu: (te[i], 0, f))
        w2s = pl.BlockSpec((1, tf, D), lambda i, f, te, nu: (te[i], f, 0))
        in_specs = [xs, w13, w13, w2s, pl.BlockSpec((tm, 1), lambda i, f, te, nu: (i, 0))]
        out_spec = pl.BlockSpec((tm, D), lambda i, f, te, nu: (i, 0))
        args = (tile_expert, n_used, x, w1, w3, w2, row_w)
        nsp = 2
    else:
        xs = pl.BlockSpec((tm, D), lambda i, f: (i, 0))
        w13 = pl.BlockSpec((1, D, tf), lambda i, f: (0, 0, f))
        w2s = pl.BlockSpec((1, tf, D), lambda i, f: (0, f, 0))
        in_specs = [xs, w13, w13, w2s]
        out_spec = pl.BlockSpec((tm, D), lambda i, f: (i, 0))
        args = (x, w1, w3, w2)
        nsp = 0
    return pl.pallas_call(
        functools.partial(_ffn_kernel, grouped=grouped),
        out_shape=jax.ShapeDtypeStruct((M, D), F32),
        grid_spec=pltpu.PrefetchScalarGridSpec(
            num_scalar_prefetch=nsp, grid=(M // tm, F // tf), in_specs=in_specs, out_specs=out_spec),
        compiler_params=_params(("arbitrary", "arbitrary")),
        name="ffn_grouped" if grouped else "ffn_dense",
    )(*args)


def _gather_kernel(src_ref, h_hbm, o_ref, buf, sem, *, rows):
    base = pl.program_id(0) * rows

    def row_copy(r):
        return pltpu.make_async_copy(h_hbm.at[pl.ds(src_ref[base + r], 1), :], buf.at[pl.ds(r, 1), :], sem)

    def issue(r, carry):
        row_copy(r).start()
        return carry

    def drain(r, carry):
        row_copy(r).wait()
        return carry

    lax.fori_loop(0, rows, issue, 0)
    lax.fori_loop(0, rows, drain, 0)
    o_ref[...] = buf[...].astype(o_ref.dtype)


def _gather_rows(h, row_src, n_rows):
    T, D = h.shape
    rows = 256
    return pl.pallas_call(
        functools.partial(_gather_kernel, rows=rows),
        out_shape=jax.ShapeDtypeStruct((n_rows, D), BF16),
        grid_spec=pltpu.PrefetchScalarGridSpec(
            num_scalar_prefetch=1, grid=(n_rows // rows,),
            in_specs=[pl.BlockSpec(memory_space=pl.ANY)],
            out_specs=pl.BlockSpec((rows, D), lambda i, src: (i, 0)),
            scratch_shapes=[pltpu.VMEM((rows, D), h.dtype), pltpu.SemaphoreType.DMA]),
        compiler_params=_params(("arbitrary",)),
        name="moe_gather",
    )(row_src, h)


def _combine_kernel(pos_ref, x_ref, gt_ref, fg_ref, ys_hbm, o_ref, buf, sem, *, rows, top_k):
    base = pl.program_id(0) * rows

    def row_copy(r, s):
        return pltpu.make_async_copy(
            ys_hbm.at[pl.ds(pos_ref[(base + r) * top_k + s], 1), :], buf.at[s, pl.ds(r, 1), :], sem)

    def issue(r, carry):
        for s in range(top_k):
            row_copy(r, s).start()
        return carry

    def drain(r, carry):
        for s in range(top_k):
            row_copy(r, s).wait()
        return carry

    lax.fori_loop(0, rows, issue, 0)
    lax.fori_loop(0, rows, drain, 0)
    y = buf[0]
    for s in range(1, top_k):
        y = y + buf[s]
    x = x_ref[...] + gt_ref[0] * y
    o_ref[...] = _rms(x, fg_ref[...])


def _combine(pos, x, gate, final_g, ys, seq, top_k):
    T, D = x.shape
    B = gate.shape[0]
    rows = 256
    return pl.pallas_call(
        functools.partial(_combine_kernel, rows=rows, top_k=top_k),
        out_shape=jax.ShapeDtypeStruct((T, D), F32),
        grid_spec=pltpu.PrefetchScalarGridSpec(
            num_scalar_prefetch=1, grid=(T // rows,),
            in_specs=[pl.BlockSpec((rows, D), lambda i, p: (i, 0)),
                      pl.BlockSpec((1, 1, D), lambda i, p: (i * rows // seq, 0, 0)),
                      pl.BlockSpec((1, D), lambda i, p: (0, 0)),
                      pl.BlockSpec(memory_space=pl.ANY)],
            out_specs=pl.BlockSpec((rows, D), lambda i, p: (i, 0)),
            scratch_shapes=[pltpu.VMEM((top_k, rows, D), F32), pltpu.SemaphoreType.DMA]),
        compiler_params=_params(("arbitrary",)),
        name="moe_combine",
    )(pos, x, gate.reshape(B, 1, D), final_g.reshape(1, D), ys)


def _route(top_i, top_w, tm, n_tiles):
    T, top_k = top_i.shape
    flat_e = top_i.reshape(-1)
    onehot = (flat_e[:, None] == jnp.arange(N_EXPERTS, dtype=jnp.int32)[None, :]).astype(jnp.int32)
    rank = jnp.sum((jnp.cumsum(onehot, axis=0) - onehot) * onehot, axis=1)
    counts = jnp.sum(onehot, axis=0)
    padded = (counts + tm - 1) // tm * tm
    ends = jnp.cumsum(padded)
    starts = ends - padded
    pos = starts[flat_e] + rank
    token = jnp.arange(T * top_k, dtype=jnp.int32) // top_k
    row_src = jnp.zeros((n_tiles * tm,), jnp.int32).at[pos].set(token)
    row_w = jnp.zeros((n_tiles * tm,), F32).at[pos].set(top_w.reshape(-1))
    tile_start = jnp.arange(n_tiles, dtype=jnp.int32) * tm
    tile_expert = jnp.minimum(jnp.sum((tile_start[:, None] >= ends[None, :]).astype(jnp.int32), axis=1),
                              N_EXPERTS - 1)
    n_used = (ends[-1] // tm).reshape(1)
    return pos.astype(jnp.int32), row_src, row_w.reshape(-1, 1), tile_expert.astype(jnp.int32), n_used.astype(jnp.int32)


def _q_up_weight(w):
    r = w.shape[0]
    w = w.reshape(r, MLA_HEADS, MLA_QK)
    nope, rp = w[:, :, :MLA_NOPE], w[:, :, MLA_NOPE:]
    half = MLA_ROPE // 2
    rot = jnp.concatenate([-rp[:, :, half:], rp[:, :, :half]], axis=2)
    z = jnp.zeros((r, MLA_HEADS, V7X_LANES - MLA_ROPE), w.dtype)
    return jnp.concatenate([nope, rp, z, rot, z], axis=2).reshape(r, -1).astype(BF16)


def _kv_up_weight(w):
    r = w.shape[0]
    w = w.reshape(r, MLA_HEADS, MLA_NOPE + MLA_V)
    return jnp.concatenate([w[:, :, :MLA_NOPE].reshape(r, -1), w[:, :, MLA_NOPE:].reshape(r, -1)],
                           axis=1).astype(BF16)


def _rope_tables(positions):
    pos = positions.reshape(-1).astype(F32)[:, None]
    inv_m = ROPE_BASE ** (-jnp.arange(0, MLA_ROPE, 2, dtype=F32) / MLA_ROPE)
    am = pos * inv_m
    zm = jnp.zeros((pos.shape[0], V7X_LANES - MLA_ROPE), F32)
    cos_m = jnp.concatenate([jnp.cos(am), jnp.cos(am), zm], axis=1)
    sin_m = jnp.concatenate([jnp.sin(am), jnp.sin(am), zm], axis=1)
    inv_r = ROPE_BASE ** (-jnp.arange(0, RET_DK, 2, dtype=F32) / RET_DK)
    ar = pos * inv_r
    cos_r = jnp.concatenate([jnp.cos(ar), jnp.cos(ar)], axis=1)
    sin_r = jnp.concatenate([-jnp.sin(ar), jnp.sin(ar)], axis=1)
    return cos_m, sin_m, cos_r, sin_r


def kernel(x, c, positions, w_ada, b_ada, ada_table, norm_g, w_in, mla_q_norm, mla_w_q_up, mla_kv_norm,
           mla_w_kv_up, hg_lb_logits, hg_norm, ret_norm, w_branch, w_out, ffn_w1, ffn_w3, ffn_w2,
           moe_router, moe_w1, moe_w3, moe_w2, final_norm):
    B, S, D = x.shape
    T = B * S
    depth = norm_g.shape[0]
    top_k = 2
    assert depth % 2 == 0, "the final norm is fused into the expert combine of the last (odd) layer"
    assert S % 512 == 0 and D % 512 == 0

    mod = _ada(c, w_ada, b_ada, ada_table)
    cos_m, sin_m, cos_r, sin_r = _rope_tables(positions)

    n_cqkv = MLA_Q_RANK + MLA_KV_RANK
    n_lat = n_cqkv + MLA_ROPE
    n_hg = 4 * HG_WIDTH
    n_ret = 2 * RET_HEADS * RET_DK + 2 * RET_HEADS * RET_DV

    xf = x.reshape(T, D)
    pending = None
    out = None
    for l in range(depth):
        shift1, scale1, gate1, shift2, scale2, gate2 = (mod[l, :, j] for j in range(N_MOD))
        if pending is None:
            (h,) = _norm(xf, norm_g[l, 0], scale1, shift1, S)
        else:
            xf, h = _norm(xf, norm_g[l, 0], scale1, shift1, S, resid=pending)
            pending = None

        lat = _matmul(h, w_in, l, 0, n_cqkv)
        kr_raw = _matmul(h, w_in, l, n_cqkv, V7X_LANES, tn=V7X_LANES)
        hg = _matmul(h, w_in, l, n_lat, n_hg)
        rt = _matmul(h, w_in, l, n_lat + n_hg, n_ret)
        gates = _matmul(h, w_in, l, n_lat + n_hg + n_ret, N_BRANCH * D, mode="sigmoid", out_dtype=BF16)

        y_mla = _mla(lat, kr_raw, cos_m, sin_m, mla_q_norm[l], _q_up_weight(mla_w_q_up[l]),
                     mla_kv_norm[l], _kv_up_weight(mla_w_kv_up[l]), B, S)
        y_hg = _hgrn2(hg, hg_lb_logits, hg_norm[l], l, B, S)
        y_ret = _retention(rt, cos_r, sin_r, ret_norm[l], B, S)

        merged = _merge(y_mla, y_hg, y_ret, w_branch, l, gates)
        xf = _matmul(merged, w_out, l, 0, D, mode="resid", resid=xf, gate=gate1, seq=S)

        if l % 2 == 0:
            (h2,) = _norm(xf, norm_g[l, 1], scale2, shift2, S)
            e = l // 2
            y = _ffn(h2, ffn_w1[e:e + 1].astype(BF16), ffn_w3[e:e + 1].astype(BF16),
                     ffn_w2[e:e + 1].astype(BF16), tm=512, tf=256)
            pending = (y, gate2)
        else:
            e = l // 2
            h2, ti, tw = _norm(xf, norm_g[l, 1], scale2, shift2, S, router=moe_router[e], h_dtype=F32)
            tm = 512
            n_tiles = T * top_k // tm + N_EXPERTS
            pos, row_src, row_w, tile_expert, n_used = _route(ti[:, :top_k], tw[:, :top_k], tm, n_tiles)
            xs = _gather_rows(h2, row_src, n_tiles * tm)
            ys = _ffn(xs, moe_w1[e].astype(BF16), moe_w3[e].astype(BF16), moe_w2[e].astype(BF16),
                      tm=tm, tf=256, tile_expert=tile_expert, n_used=n_used, row_w=row_w)
            assert l == depth - 1
            out = _combine(pos, xf, gate2, final_norm, ys, S, top_k)
    return out.reshape(B, S, D)
```

```python
import functools
import math

import jax
import jax.numpy as jnp
from jax import lax
from jax.experimental import pallas as pl
from jax.experimental.pallas import tpu as pltpu

F32 = jnp.float32
BF16 = jnp.bfloat16

V7X_LANES = 128
V7X_SUBLANES = 8
V7X_VMEM_BYTES = 64 * 1024 * 1024
V7X_VMEM_LIMIT = V7X_VMEM_BYTES - 8 * 1024 * 1024

EPS = 1e-6
ROPE_BASE = 10000.0
MASK_VALUE = -1e30

MLA_HEADS = 16
MLA_NOPE = 128
MLA_ROPE = 64
MLA_V = 128
MLA_Q_RANK = 1024
MLA_KV_RANK = 512
MLA_QK = MLA_NOPE + MLA_ROPE
MLA_QPAD = 2 * V7X_LANES

HG_HEADS = 16
HG_D = 128
HG_WIDTH = HG_HEADS * HG_D
HG_CHUNK = 64
HG_SUB = 8
HG_MIN_F = 1e-6

RET_HEADS = 8
RET_DK = 128
RET_DV = 256
RET_CHUNK = 128

N_BRANCH = 3
N_EXPERTS = 8
N_MOD = 6


def _params(semantics):
    return pltpu.CompilerParams(dimension_semantics=semantics, vmem_limit_bytes=V7X_VMEM_LIMIT)


def _rms(x, gain):
    return x * lax.rsqrt(jnp.mean(x * x, axis=-1, keepdims=True) + EPS) * gain


def _silu(x):
    return x * jax.nn.sigmoid(x)


def _dot_nt(a, b):
    return lax.dot_general(a, b, (((1,), (1,)), ((), ())), preferred_element_type=F32)


def _dot_tn(a, b):
    return jnp.dot(a.T, b, preferred_element_type=F32)


def _ada_kernel(c_ref, w_ref, b_ref, tab_ref, o_ref, acc_ref):
    k = pl.program_id(0)

    @pl.when(k == 0)
    def _():
        acc_ref[...] = jnp.zeros_like(acc_ref)

    s = _silu(c_ref[...]).astype(BF16)
    acc_ref[...] += jnp.dot(s, w_ref[...].astype(BF16), preferred_element_type=F32)

    @pl.when(k == pl.num_programs(0) - 1)
    def _():
        acc = acc_ref[...] + b_ref[...]
        for l in range(o_ref.shape[0]):
            o_ref[l] = acc + tab_ref[l]


def _ada(c, w_ada, b_ada, ada_table):
    B, D = c.shape
    depth = ada_table.shape[0]
    N = w_ada.shape[1]
    tk = V7X_LANES
    cp = jnp.zeros((V7X_SUBLANES, D), F32).at[:B].set(c)
    out = pl.pallas_call(
        _ada_kernel,
        out_shape=jax.ShapeDtypeStruct((depth, V7X_SUBLANES, N), F32),
        grid=(D // tk,),
        in_specs=[
            pl.BlockSpec((V7X_SUBLANES, tk), lambda k: (0, k)),
            pl.BlockSpec((tk, N), lambda k: (k, 0)),
            pl.BlockSpec((1, N), lambda k: (0, 0)),
            pl.BlockSpec((depth, 1, N), lambda k: (0, 0, 0)),
        ],
        out_specs=pl.BlockSpec((depth, V7X_SUBLANES, N), lambda k: (0, 0, 0)),
        scratch_shapes=[pltpu.VMEM((V7X_SUBLANES, N), F32)],
        compiler_params=_params(("arbitrary",)),
        name="ada_mod",
    )(cp, w_ada, b_ada.reshape(1, N), ada_table.reshape(depth, 1, N))
    return out[:, :B].reshape(depth, B, N_MOD, D)


def _norm_kernel(*refs, has_resid, has_router, n_experts):
    it = iter(refs)
    x_ref = next(it)
    if has_resid:
        y_ref, gt_ref = next(it), next(it)
    g_ref, sc_ref, sh_ref = next(it), next(it), next(it)
    if has_router:
        r_ref = next(it)
    if has_resid:
        xo_ref = next(it)
    h_ref = next(it)
    if has_router:
        ti_ref, tw_ref = next(it), next(it)

    x = x_ref[...]
    if has_resid:
        x = x + gt_ref[0] * y_ref[...]
        xo_ref[...] = x
    h = _rms(x, g_ref[...]) * (1.0 + sc_ref[0]) + sh_ref[0]
    h_ref[...] = h.astype(h_ref.dtype)
    if has_router:
        logits = jnp.dot(h, r_ref[...], preferred_element_type=F32, precision=lax.Precision.HIGHEST)
        lane = lax.broadcasted_iota(jnp.int32, logits.shape, 1).astype(F32)
        neg = jnp.float32(-jnp.inf)
        big = jnp.float32(V7X_LANES)
        lg = jnp.where(lane < n_experts, logits, neg)
        m1 = jnp.max(lg, axis=-1, keepdims=True)
        i1 = jnp.min(jnp.where(lg == m1, lane, big), axis=-1, keepdims=True)
        lg2 = jnp.where(lane == i1, neg, lg)
        m2 = jnp.max(lg2, axis=-1, keepdims=True)
        i2 = jnp.min(jnp.where(lg2 == m2, lane, big), axis=-1, keepdims=True)
        e = jnp.exp(m2 - m1)
        w1 = 1.0 / (1.0 + e)
        w2 = e / (1.0 + e)
        ti_ref[...] = jnp.where(lane == 0.0, i1, jnp.where(lane == 1.0, i2, 0.0)).astype(jnp.int32)
        tw_ref[...] = jnp.where(lane == 0.0, w1, jnp.where(lane == 1.0, w2, 0.0))


def _norm(x, gain, scale, shift, seq, *, resid=None, router=None, h_dtype=BF16):
    T, D = x.shape
    B = scale.shape[0]
    tm = 256
    row = pl.BlockSpec((tm, D), lambda i: (i, 0))
    per_batch = pl.BlockSpec((1, 1, D), lambda i: (i * tm // seq, 0, 0))
    args, in_specs = [x], [row]
    if resid is not None:
        args += [resid[0], resid[1].reshape(B, 1, D)]
        in_specs += [row, per_batch]
    args += [gain.reshape(1, D), scale.reshape(B, 1, D), shift.reshape(B, 1, D)]
    in_specs += [pl.BlockSpec((1, D), lambda i: (0, 0)), per_batch, per_batch]
    out_shape, out_specs = [], []
    if router is not None:
        n_experts = router.shape[1]
        rp = jnp.zeros((D, V7X_LANES), F32).at[:, :n_experts].set(router)
        args.append(rp)
        in_specs.append(pl.BlockSpec((D, V7X_LANES), lambda i: (0, 0)))
    else:
        n_experts = 0
    if resid is not None:
        out_shape.append(jax.ShapeDtypeStruct((T, D), F32))
        out_specs.append(row)
    out_shape.append(jax.ShapeDtypeStruct((T, D), h_dtype))
    out_specs.append(row)
    if router is not None:
        small = pl.BlockSpec((tm, V7X_LANES), lambda i: (i, 0))
        out_shape += [jax.ShapeDtypeStruct((T, V7X_LANES), jnp.int32), jax.ShapeDtypeStruct((T, V7X_LANES), F32)]
        out_specs += [small, small]
    return pl.pallas_call(
        functools.partial(_norm_kernel, has_resid=resid is not None, has_router=router is not None,
                          n_experts=n_experts),
        out_shape=out_shape,
        grid=(T // tm,),
        in_specs=in_specs,
        out_specs=out_specs,
        compiler_params=_params(("arbitrary",)),
        name="ada_norm",
    )(*args)


_W_CAST_ROWS = 512


_WT_GROUP = 64


def _mm_kernel(*refs, mode, n_groups):
    it = iter(refs)
    a_ref = next(it)
    w_refs = [next(it) for _ in range(max(n_groups, 1))]
    if mode == "resid":
        r_ref, g_ref = next(it), next(it)
    o_ref, wscr = next(it), next(it)

    @pl.when(pl.program_id(1) == 0)
    def _():
        if n_groups:
            for g, w_ref in enumerate(w_refs):
                wscr[g * _WT_GROUP:(g + 1) * _WT_GROUP, :] = w_ref[0, 0].astype(BF16)
        else:
            for r0 in range(0, wscr.shape[0], _W_CAST_ROWS):
                wscr[r0:r0 + _W_CAST_ROWS, :] = w_refs[0][0, r0:r0 + _W_CAST_ROWS, :].astype(BF16)

    if n_groups:
        acc = _dot_nt(a_ref[...], wscr[...])
    else:
        acc = jnp.dot(a_ref[...], wscr[...], preferred_element_type=F32)
    if mode == "plain":
        o_ref[...] = acc.astype(o_ref.dtype)
    elif mode == "sigmoid":
        o_ref[...] = jax.nn.sigmoid(acc).astype(o_ref.dtype)
    else:
        o_ref[...] = r_ref[...] + g_ref[0] * acc


def _matmul(a, w, layer, col0, n, *, transposed=False, mode="plain", out_dtype=F32, resid=None, gate=None,
            seq=None, tm=1024, tn=512):
    M, K = a.shape
    assert n % tn == 0 and M % tm == 0
    args = [a]
    in_specs = [pl.BlockSpec((tm, K), lambda j, i: (i, 0))]
    if transposed:
        assert col0 % _WT_GROUP == 0 and tn % _WT_GROUP == 0
        n_groups = tn // _WT_GROUP
        g0 = col0 // _WT_GROUP
        for g in range(n_groups):
            args.append(w)
            in_specs.append(pl.BlockSpec((1, 1, _WT_GROUP, K),
                                         lambda j, i, g=g: (layer, g0 + j * n_groups + g, 0, 0)))
        wscr = pltpu.VMEM((tn, K), BF16)
    else:
        assert col0 % tn == 0 and K % _W_CAST_ROWS == 0
        n_groups = 0
        args.append(w)
        in_specs.append(pl.BlockSpec((1, K, tn), lambda j, i: (layer, 0, col0 // tn + j)))
        wscr = pltpu.VMEM((K, tn), BF16)
    if mode == "resid":
        B = gate.shape[0]
        args += [resid, gate.reshape(B, 1, n)]
        in_specs += [pl.BlockSpec((tm, tn), lambda j, i: (i, j)),
                     pl.BlockSpec((1, 1, tn), lambda j, i: (i * tm // seq, 0, j))]
    return pl.pallas_call(
        functools.partial(_mm_kernel, mode=mode, n_groups=n_groups),
        out_shape=jax.ShapeDtypeStruct((M, n), out_dtype),
        grid=(n // tn, M // tm),
        in_specs=in_specs,
        out_specs=pl.BlockSpec((tm, tn), lambda j, i: (i, j)),
        scratch_shapes=[wscr],
        compiler_params=_params(("arbitrary", "arbitrary")),
        name="matmul_" + mode,
    )(*args)


def _mla_q_kernel(cq_ref, g_ref, w_ref, cos_ref, sin_ref, o_ref, a_scr, *, heads_per_step, scale):
    @pl.when(pl.program_id(1) == 0)
    def _():
        a_scr[...] = _rms(cq_ref[...], g_ref[...]).astype(BF16)

    acc = jnp.dot(a_scr[...], w_ref[...], preferred_element_type=F32)
    cos, sin = cos_ref[...], sin_ref[...]
    L = V7X_LANES
    for h in range(heads_per_step):
        nope = acc[:, 3 * L * h:3 * L * h + L]
        rp = acc[:, 3 * L * h + L:3 * L * h + 2 * L]
        rt = acc[:, 3 * L * h + 2 * L:3 * L * h + 3 * L]
        o_ref[:, 2 * L * h:2 * L * h + L] = (nope * scale).astype(BF16)
        o_ref[:, 2 * L * h + L:2 * L * h + 2 * L] = ((rp * cos + rt * sin) * scale).astype(BF16)


def _mla_kv_kernel(ckv_ref, g_ref, w_ref, kraw_ref, cos_ref, sin_ref, kv_ref, kr_ref, a_scr):
    @pl.when(pl.program_id(1) == 0)
    def _():
        a_scr[...] = _rms(ckv_ref[...], g_ref[...]).astype(BF16)
        x = kraw_ref[...]
        half = MLA_ROPE // 2
        lane = lax.broadcasted_iota(jnp.int32, x.shape, 1)
        swapped = jnp.where(lane < half, -pltpu.roll(x, V7X_LANES - half, 1), pltpu.roll(x, half, 1))
        roped = x * cos_ref[...] + swapped * sin_ref[...]
        kr_ref[...] = jnp.where(lane < MLA_ROPE, roped, 0.0).astype(BF16)

    kv_ref[...] = jnp.dot(a_scr[...], w_ref[...], preferred_element_type=F32).astype(BF16)


def _mla_attn_kernel(q_ref, k_ref, kr_ref, v_ref, o_ref, *, tq, n_heads):
    i = pl.program_id(2)
    qs = [q_ref[:, h * MLA_QPAD:(h + 1) * MLA_QPAD] for h in range(n_heads)]

    def block(j, carry, masked):
        off = pl.multiple_of(j * tq, tq)
        kr = kr_ref[pl.ds(off, tq), :]
        out = []
        for h in range(n_heads):
            m, l, acc = carry[h]
            kk = jnp.concatenate([k_ref[pl.ds(off, tq), h * MLA_NOPE:(h + 1) * MLA_NOPE], kr], axis=1)
            s = _dot_nt(qs[h], kk)
            if masked:
                r = lax.broadcasted_iota(jnp.int32, s.shape, 0)
                c = lax.broadcasted_iota(jnp.int32, s.shape, 1)
                s = jnp.where(c <= r, s, MASK_VALUE)
            m_new = jnp.maximum(m, jnp.max(s, axis=-1, keepdims=True))
            alpha = jnp.exp(m - m_new)
            p = jnp.exp(s - m_new)
            l = alpha * l + jnp.sum(p, axis=-1, keepdims=True)
            v = v_ref[pl.ds(off, tq), h * MLA_V:(h + 1) * MLA_V]
            acc = alpha * acc + jnp.dot(p.astype(BF16), v, preferred_element_type=F32)
            out.append((m_new, l, acc))
        return tuple(out)

    carry = tuple((jnp.full((tq, 1), MASK_VALUE, F32), jnp.zeros((tq, 1), F32), jnp.zeros((tq, MLA_V), F32))
                  for _ in range(n_heads))
    carry = lax.fori_loop(0, i, lambda j, c: block(j, c, False), carry)
    carry = block(i, carry, True)
    for h in range(n_heads):
        _, l, acc = carry[h]
        o_ref[:, h * MLA_V:(h + 1) * MLA_V] = (acc / l).astype(o_ref.dtype)


def _mla(lat, kr_raw, cos_m, sin_m, q_norm, wq, kv_norm, wkv, B, S):
    T = lat.shape[0]
    L = V7X_LANES
    tm = 512
    hps = 4
    scale = MLA_QK ** -0.5
    q = pl.pallas_call(
        functools.partial(_mla_q_kernel, heads_per_step=hps, scale=scale),
        out_shape=jax.ShapeDtypeStruct((T, MLA_HEADS * MLA_QPAD), BF16),
        grid=(T // tm, MLA_HEADS // hps),
        in_specs=[
            pl.BlockSpec((tm, MLA_Q_RANK), lambda i, j: (i, 0)),
            pl.BlockSpec((1, MLA_Q_RANK), lambda i, j: (0, 0)),
            pl.BlockSpec((MLA_Q_RANK, hps * 3 * L), lambda i, j: (0, j)),
            pl.BlockSpec((tm, L), lambda i, j: (i, 0)),
            pl.BlockSpec((tm, L), lambda i, j: (i, 0)),
        ],
        out_specs=pl.BlockSpec((tm, hps * MLA_QPAD), lambda i, j: (i, j)),
        scratch_shapes=[pltpu.VMEM((tm, MLA_Q_RANK), BF16)],
        compiler_params=_params(("arbitrary", "arbitrary")),
        name="mla_q",
    )(lat, q_norm.reshape(1, -1), wq, cos_m, sin_m)

    tn = 1024
    kv_w = 2 * MLA_HEADS * MLA_NOPE
    ckv_blk = MLA_Q_RANK // MLA_KV_RANK
    kv, kr = pl.pallas_call(
        _mla_kv_kernel,
        out_shape=[jax.ShapeDtypeStruct((T, kv_w), BF16), jax.ShapeDtypeStruct((T, L), BF16)],
        grid=(T // tm, kv_w // tn),
        in_specs=[
            pl.BlockSpec((tm, MLA_KV_RANK), lambda i, j: (i, ckv_blk)),
            pl.BlockSpec((1, MLA_KV_RANK), lambda i, j: (0, 0)),
            pl.BlockSpec((MLA_KV_RANK, tn), lambda i, j: (0, j)),
            pl.BlockSpec((tm, L), lambda i, j: (i, 0)),
            pl.BlockSpec((tm, L), lambda i, j: (i, 0)),
            pl.BlockSpec((tm, L), lambda i, j: (i, 0)),
        ],
        out_specs=[pl.BlockSpec((tm, tn), lambda i, j: (i, j)), pl.BlockSpec((tm, L), lambda i, j: (i, 0))],
        scratch_shapes=[pltpu.VMEM((tm, MLA_KV_RANK), BF16)],
        compiler_params=_params(("arbitrary", "arbitrary")),
        name="mla_kv",
    )(lat, kv_norm.reshape(1, -1), wkv, kr_raw, cos_m, sin_m)

    tq = 512
    nq = S // tq
    nh = 2
    hg = MLA_HEADS // nh
    return pl.pallas_call(
        functools.partial(_mla_attn_kernel, tq=tq, n_heads=nh),
        out_shape=jax.ShapeDtypeStruct((T, MLA_HEADS * MLA_V), BF16),
        grid=(B, hg, nq),
        in_specs=[
            pl.BlockSpec((tq, nh * MLA_QPAD), lambda b, h, i: (b * nq + i, h)),
            pl.BlockSpec((S, nh * MLA_NOPE), lambda b, h, i: (b, h)),
            pl.BlockSpec((S, L), lambda b, h, i: (b, 0)),
            pl.BlockSpec((S, nh * MLA_V), lambda b, h, i: (b, hg + h)),
        ],
        out_specs=pl.BlockSpec((tq, nh * MLA_V), lambda b, h, i: (b * nq + i, h)),
        compiler_params=_params(("arbitrary", "arbitrary", "arbitrary")),
        name="mla_attn",
    )(q, kv, kr, kv)


def _hgrn2_kernel(q_ref, f_ref, i_ref, gate_ref, lbl_ref, ng_ref, o_ref, st_scr, q_scr, k_scr, b_scr, *, layer):
    tb = q_ref.shape[0]
    C = HG_CHUNK

    @pl.when(pl.program_id(2) == 0)
    def _():
        st_scr[...] = jnp.zeros_like(st_scr)

    lg = lbl_ref[...]
    e = jnp.exp(lg - jnp.max(lg, axis=0, keepdims=True))
    p = e / jnp.sum(e, axis=0, keepdims=True)
    lb = jnp.zeros((1, HG_D), F32)
    for l in range(1, layer + 1):
        lb = lb + p[l:l + 1, :]

    fr = f_ref[...]
    f = lb + (1.0 - lb) * jax.nn.sigmoid(fr)
    logf = jnp.log(jnp.maximum(f, HG_MIN_F))
    q_scr[...] = _silu(q_ref[...])
    k_scr[...] = (1.0 - lb) * jax.nn.sigmoid(-fr)

    hi = logf.astype(BF16)
    r1 = logf - hi.astype(F32)
    mid = r1.astype(BF16)
    lo = (r1 - mid.astype(F32)).astype(BF16)
    rr = lax.broadcasted_iota(jnp.int32, (C, C), 0)
    cc = lax.broadcasted_iota(jnp.int32, (C, C), 1)
    causal = cc <= rr
    tril = jnp.where(causal, 1.0, 0.0).astype(BF16)
    for c in range(tb // C):
        sl = slice(c * C, (c + 1) * C)
        parts = jnp.concatenate([hi[sl], mid[sl], lo[sl]], axis=1)
        cs = jnp.dot(tril, parts, preferred_element_type=F32)
        b_scr[sl, :] = cs[:, :HG_D] + cs[:, HG_D:2 * HG_D] + cs[:, 2 * HG_D:]

    st = st_scr[...]
    ng = ng_ref[...]
    for c in range(tb // C):
        r0 = c * C
        b_c = b_scr[r0:r0 + C, :]
        q_c = q_scr[r0:r0 + C, :]
        k_c = k_scr[r0:r0 + C, :]
        v_c = i_ref[r0:r0 + C, :]
        b_last = b_scr[r0 + C - 1:r0 + C, :]
        rows = []
        for blk in range(C // HG_SUB):
            s0 = r0 + blk * HG_SUB
            n = (blk + 1) * HG_SUB
            ref = b_scr[s0 + HG_SUB // 2 - 1:s0 + HG_SUB // 2, :]
            q_b = q_scr[s0:s0 + HG_SUB, :] * jnp.exp(b_scr[s0:s0 + HG_SUB, :] - ref)
            k_b = k_scr[r0:r0 + n, :] * jnp.exp(ref - b_scr[r0:r0 + n, :])
            if n < C:
                k_b = jnp.concatenate([k_b, jnp.zeros((C - n, HG_D), F32)], axis=0)
            rows.append(_dot_nt(q_b, k_b))
        scores = jnp.where(causal, jnp.concatenate(rows, axis=0), 0.0)
        o = jnp.dot(scores, v_c, preferred_element_type=F32) + _dot_nt(q_c * jnp.exp(b_c), st)
        st = st * jnp.exp(b_last) + _dot_tn(v_c, k_c * jnp.exp(b_last - b_c))
        y = _rms(o, ng) * _silu(gate_ref[r0:r0 + C, :])
        o_ref[r0:r0 + C, :] = y.astype(o_ref.dtype)
    st_scr[...] = st


def _hgrn2(hg, lb_logits, norm_g, layer, B, S):
    T = hg.shape[0]
    tb = 512
    nb = S // tb
    depth = lb_logits.shape[0]

    def part(p):
        return pl.BlockSpec((tb, HG_D), lambda b, h, j: (b * nb + j, p * HG_HEADS + h))

    return pl.pallas_call(
        functools.partial(_hgrn2_kernel, layer=layer),
        out_shape=jax.ShapeDtypeStruct((T, HG_WIDTH), BF16),
        grid=(B, HG_HEADS, nb),
        in_specs=[part(0), part(1), part(2), part(3),
                  pl.BlockSpec((depth, HG_D), lambda b, h, j: (0, h)),
                  pl.BlockSpec((1, HG_D), lambda b, h, j: (0, h))],
        out_specs=pl.BlockSpec((tb, HG_D), lambda b, h, j: (b * nb + j, h)),
        scratch_shapes=[pltpu.VMEM((HG_D, HG_D), F32), pltpu.VMEM((tb, HG_D), F32),
                        pltpu.VMEM((tb, HG_D), F32), pltpu.VMEM((tb, HG_D), F32)],
        compiler_params=_params(("arbitrary", "arbitrary", "arbitrary")),
        name="hgrn2",
    )(hg, hg, hg, hg, lb_logits, norm_g.reshape(1, -1))


def _ret_kernel(q_ref, k_ref, v_ref, gate_ref, cos_ref, sin_ref, lg_ref, ng_ref, o_ref, st_scr):
    tb = q_ref.shape[0]
    C = RET_CHUNK

    @pl.when(pl.program_id(2) == 0)
    def _():
        st_scr[...] = jnp.zeros_like(st_scr)

    cos, sin = cos_ref[...], sin_ref[...]
    half = RET_DK // 2
    xq, xk = q_ref[...], k_ref[...]
    q = xq * cos + pltpu.roll(xq, half, 1) * sin
    k = (xk * cos + pltpu.roll(xk, half, 1) * sin) * (RET_DK ** -0.5)

    lg = lg_ref[0]
    rr = lax.broadcasted_iota(jnp.int32, (C, C), 0)
    cc = lax.broadcasted_iota(jnp.int32, (C, C), 1)
    rel = (rr - cc).astype(F32)
    decay = jnp.where(rel >= 0.0, jnp.exp(lg * jnp.maximum(rel, 0.0)), 0.0)
    idx = rr.astype(F32)
    q_dec = jnp.exp(lg * (idx + 1.0))
    k_dec = jnp.exp(lg * (C - 1.0 - idx))
    chunk_dec = jnp.exp(lg * float(C))

    st = st_scr[...]
    ng = ng_ref[...]
    for c in range(tb // C):
        sl = slice(c * C, (c + 1) * C)
        q_c, k_c = q[sl], k[sl]
        v_c = v_ref[sl, :]
        s = _dot_nt(q_c.astype(BF16), k_c.astype(BF16)) * decay
        o = jnp.dot(s.astype(BF16), v_c.astype(BF16), preferred_element_type=F32)
        o = o + _dot_nt((q_c * q_dec).astype(BF16), st.astype(BF16))
        st = st * chunk_dec + jnp.dot(v_c.T.astype(BF16), (k_c * k_dec).astype(BF16),
                                      preferred_element_type=F32)
        y = _rms(o, ng) * _silu(gate_ref[sl, :])
        o_ref[sl, :] = y.astype(o_ref.dtype)
    st_scr[...] = st


def _retention(rt, cos_r, sin_r, norm_g, B, S):
    T = rt.shape[0]
    tb = 512
    nb = S // tb
    H = RET_HEADS
    v_blk0 = 2 * H * RET_DK // RET_DV
    log_gamma = jnp.log1p(-(2.0 ** (-5.0 - jnp.arange(H, dtype=F32))))
    lg = jnp.broadcast_to(log_gamma[:, None, None], (H, 1, RET_CHUNK))
    return pl.pallas_call(
        _ret_kernel,
        out_shape=jax.ShapeDtypeStruct((T, H * RET_DV), BF16),
        grid=(B, H, nb),
        in_specs=[
            pl.BlockSpec((tb, RET_DK), lambda b, h, j: (b * nb + j, h)),
            pl.BlockSpec((tb, RET_DK), lambda b, h, j: (b * nb + j, H + h)),
            pl.BlockSpec((tb, RET_DV), lambda b, h, j: (b * nb + j, v_blk0 + h)),
            pl.BlockSpec((tb, RET_DV), lambda b, h, j: (b * nb + j, v_blk0 + H + h)),
            pl.BlockSpec((tb, RET_DK), lambda b, h, j: (b * nb + j, 0)),
            pl.BlockSpec((tb, RET_DK), lambda b, h, j: (b * nb + j, 0)),
            pl.BlockSpec((1, 1, RET_CHUNK), lambda b, h, j: (h, 0, 0)),
            pl.BlockSpec((1, RET_DV), lambda b, h, j: (0, h)),
        ],
        out_specs=pl.BlockSpec((tb, RET_DV), lambda b, h, j: (b * nb + j, h)),
        scratch_shapes=[pltpu.VMEM((RET_DV, RET_DK), F32)],
        compiler_params=_params(("arbitrary", "arbitrary", "arbitrary")),
        name="retention",
    )(rt, rt, rt, rt, cos_r, sin_r, lg, norm_g.reshape(1, -1))


def _merge_kernel(ya_ref, yb_ref, yc_ref, w_ref, ga_ref, gb_ref, gc_ref, o_ref, wscr):
    @pl.when(pl.program_id(1) == 0)
    def _():
        for br in range(N_BRANCH):
            for r0 in range(0, wscr.shape[1], _W_CAST_ROWS):
                wscr[br, r0:r0 + _W_CAST_ROWS, :] = w_ref[0, br, r0:r0 + _W_CAST_ROWS, :].astype(BF16)

    acc = ga_ref[...].astype(F32) * jnp.dot(ya_ref[...], wscr[0], preferred_element_type=F32)
    acc = acc + gb_ref[...].astype(F32) * jnp.dot(yb_ref[...], wscr[1], preferred_element_type=F32)
    acc = acc + gc_ref[...].astype(F32) * jnp.dot(yc_ref[...], wscr[2], preferred_element_type=F32)
    o_ref[...] = acc.astype(o_ref.dtype)


def _merge(ya, yb, yc, w_branch, layer, gates):
    T, K = ya.shape
    D = w_branch.shape[3]
    tm, tn = 512, 512
    nj = D // tn
    ybs = pl.BlockSpec((tm, K), lambda j, i: (i, 0))
    return pl.pallas_call(
        _merge_kernel,
        out_shape=jax.ShapeDtypeStruct((T, D), BF16),
        grid=(nj, T // tm),
        in_specs=[ybs, ybs, ybs,
                  pl.BlockSpec((1, N_BRANCH, K, tn), lambda j, i: (layer, 0, 0, j)),
                  pl.BlockSpec((tm, tn), lambda j, i: (i, j)),
                  pl.BlockSpec((tm, tn), lambda j, i: (i, nj + j)),
                  pl.BlockSpec((tm, tn), lambda j, i: (i, 2 * nj + j))],
        out_specs=pl.BlockSpec((tm, tn), lambda j, i: (i, j)),
        scratch_shapes=[pltpu.VMEM((N_BRANCH, K, tn), BF16)],
        compiler_params=_params(("arbitrary", "arbitrary")),
        name="merge",
    )(ya, yb, yc, w_branch, gates, gates, gates)


def _ffn_kernel(x_ref, w1_ref, w3_ref, w2_ref, o_ref):
    @pl.when(pl.program_id(1) == 0)
    def _():
        o_ref[...] = jnp.zeros_like(o_ref)

    x = x_ref[...]
    g = jnp.dot(x, w1_ref[0], preferred_element_type=F32)
    u = jnp.dot(x, w3_ref[0], preferred_element_type=F32)
    o_ref[...] += jnp.dot((_silu(g) * u).astype(BF16), w2_ref[0], preferred_element_type=F32)


def _ffn(x, w1, w3, w2, e, *, tm, tf):
    M, D = x.shape
    F = w1.shape[2]
    assert M % tm == 0 and F % tf == 0
    w13 = pl.BlockSpec((1, D, tf), lambda i, f: (e, 0, f))
    return pl.pallas_call(
        _ffn_kernel,
        out_shape=jax.ShapeDtypeStruct((M, D), F32),
        grid=(M // tm, F // tf),
        in_specs=[pl.BlockSpec((tm, D), lambda i, f: (i, 0)), w13, w13,
                  pl.BlockSpec((1, tf, D), lambda i, f: (e, f, 0))],
        out_specs=pl.BlockSpec((tm, D), lambda i, f: (i, 0)),
        compiler_params=_params(("arbitrary", "arbitrary")),
        name="ffn_dense",
    )(x, w1, w3, w2)


def _expert_changed(te_ref, i):
    return jnp.logical_or(i == 0, te_ref[i] != te_ref[jnp.maximum(i - 1, 0)])


def _cast_rows(src_ref, dst_ref):
    for r0 in range(0, dst_ref.shape[0], _W_CAST_ROWS):
        dst_ref[r0:r0 + _W_CAST_ROWS, :] = src_ref[0, 0, r0:r0 + _W_CAST_ROWS, :].astype(BF16)


def _moe_up_kernel(te_ref, nu_ref, x_ref, w1_ref, w3_ref, o_ref, w1s, w3s):
    i = pl.program_id(1)

    @pl.when(_expert_changed(te_ref, i))
    def _():
        _cast_rows(w1_ref, w1s)
        _cast_rows(w3_ref, w3s)

    @pl.when(i < nu_ref[0])
    def _():
        x = x_ref[...]
        g = jnp.dot(x, w1s[...], preferred_element_type=F32)
        u = jnp.dot(x, w3s[...], preferred_element_type=F32)
        o_ref[...] = (_silu(g) * u).astype(o_ref.dtype)

    @pl.when(i >= nu_ref[0])
    def _():
        o_ref[...] = jnp.zeros_like(o_ref)


def _moe_down_kernel(te_ref, nu_ref, h_ref, w2_ref, rw_ref, o_ref, w2s):
    i = pl.program_id(1)

    @pl.when(_expert_changed(te_ref, i))
    def _():
        _cast_rows(w2_ref, w2s)

    @pl.when(i < nu_ref[0])
    def _():
        o_ref[...] = jnp.dot(h_ref[...], w2s[...], preferred_element_type=F32) * rw_ref[...]

    @pl.when(i >= nu_ref[0])
    def _():
        o_ref[...] = jnp.zeros_like(o_ref)


def _moe_experts(xs, w1, w3, w2, e, tile_expert, n_used, row_w, *, tm, tf, tn):
    P, D = xs.shape
    F = w1.shape[3]
    assert P % tm == 0 and F % tf == 0 and D % tn == 0 and D % _W_CAST_ROWS == 0 and F % _W_CAST_ROWS == 0
    n_tiles = P // tm
    hmid = pl.pallas_call(
        _moe_up_kernel,
        out_shape=jax.ShapeDtypeStruct((P, F), BF16),
        grid_spec=pltpu.PrefetchScalarGridSpec(
            num_scalar_prefetch=2, grid=(F // tf, n_tiles),
            in_specs=[pl.BlockSpec((tm, D), lambda f, i, te, nu: (i, 0)),
                      pl.BlockSpec((1, 1, D, tf), lambda f, i, te, nu: (e, te[i], 0, f)),
                      pl.BlockSpec((1, 1, D, tf), lambda f, i, te, nu: (e, te[i], 0, f))],
            out_specs=pl.BlockSpec((tm, tf), lambda f, i, te, nu: (i, f)),
            scratch_shapes=[pltpu.VMEM((D, tf), BF16), pltpu.VMEM((D, tf), BF16)]),
        compiler_params=_params(("arbitrary", "arbitrary")),
        name="moe_up",
    )(tile_expert, n_used, xs, w1, w3)
    return pl.pallas_call(
        _moe_down_kernel,
        out_shape=jax.ShapeDtypeStruct((P, D), F32),
        grid_spec=pltpu.PrefetchScalarGridSpec(
            num_scalar_prefetch=2, grid=(D // tn, n_tiles),
            in_specs=[pl.BlockSpec((tm, F), lambda n, i, te, nu: (i, 0)),
                      pl.BlockSpec((1, 1, F, tn), lambda n, i, te, nu: (e, te[i], 0, n)),
                      pl.BlockSpec((tm, 1), lambda n, i, te, nu: (i, 0))],
            out_specs=pl.BlockSpec((tm, tn), lambda n, i, te, nu: (i, n)),
            scratch_shapes=[pltpu.VMEM((F, tn), BF16)]),
        compiler_params=_params(("arbitrary", "arbitrary")),
        name="moe_down",
    )(tile_expert, n_used, hmid, w2, row_w)


def _gather_kernel(src_ref, h_hbm, o_ref, buf, sem, *, rows):
    base = pl.program_id(0) * rows

    def row_copy(r):
        return pltpu.make_async_copy(h_hbm.at[pl.ds(src_ref[base + r], 1), :], buf.at[pl.ds(r, 1), :], sem)

    def issue(r, carry):
        row_copy(r).start()
        return carry

    def drain(r, carry):
        row_copy(r).wait()
        return carry

    lax.fori_loop(0, rows, issue, 0)
    lax.fori_loop(0, rows, drain, 0)
    o_ref[...] = buf[...].astype(o_ref.dtype)


def _gather_rows(h, row_src, n_rows):
    T, D = h.shape
    rows = 256
    return pl.pallas_call(
        functools.partial(_gather_kernel, rows=rows),
        out_shape=jax.ShapeDtypeStruct((n_rows, D), BF16),
        grid_spec=pltpu.PrefetchScalarGridSpec(
            num_scalar_prefetch=1, grid=(n_rows // rows,),
            in_specs=[pl.BlockSpec(memory_space=pl.ANY)],
            out_specs=pl.BlockSpec((rows, D), lambda i, src: (i, 0)),
            scratch_shapes=[pltpu.VMEM((rows, D), h.dtype), pltpu.SemaphoreType.DMA]),
        compiler_params=_params(("arbitrary",)),
        name="moe_gather",
    )(row_src, h)


def _combine_kernel(pos_ref, x_ref, gt_ref, fg_ref, ys_hbm, o_ref, buf, sem, *, rows, top_k):
    base = pl.program_id(0) * rows

    def row_copy(r, s):
        return pltpu.make_async_copy(
            ys_hbm.at[pl.ds(pos_ref[(base + r) * top_k + s], 1), :], buf.at[s, pl.ds(r, 1), :], sem)

    def issue(r, carry):
        for s in range(top_k):
            row_copy(r, s).start()
        return carry

    def drain(r, carry):
        for s in range(top_k):
            row_copy(r, s).wait()
        return carry

    lax.fori_loop(0, rows, issue, 0)
    lax.fori_loop(0, rows, drain, 0)
    y = buf[0]
    for s in range(1, top_k):
        y = y + buf[s]
    x = x_ref[...] + gt_ref[0] * y
    o_ref[...] = _rms(x, fg_ref[...])


def _combine(pos, x, gate, final_g, ys, seq, top_k):
    T, D = x.shape
    B = gate.shape[0]
    rows = 256
    return pl.pallas_call(
        functools.partial(_combine_kernel, rows=rows, top_k=top_k),
        out_shape=jax.ShapeDtypeStruct((T, D), F32),
        grid_spec=pltpu.PrefetchScalarGridSpec(
            num_scalar_prefetch=1, grid=(T // rows,),
            in_specs=[pl.BlockSpec((rows, D), lambda i, p: (i, 0)),
                      pl.BlockSpec((1, 1, D), lambda i, p: (i * rows // seq, 0, 0)),
                      pl.BlockSpec((1, D), lambda i, p: (0, 0)),
                      pl.BlockSpec(memory_space=pl.ANY)],
            out_specs=pl.BlockSpec((rows, D), lambda i, p: (i, 0)),
            scratch_shapes=[pltpu.VMEM((top_k, rows, D), F32), pltpu.SemaphoreType.DMA]),
        compiler_params=_params(("arbitrary",)),
        name="moe_combine",
    )(pos, x, gate.reshape(B, 1, D), final_g.reshape(1, D), ys)


def _route(top_i, top_w, tm, n_tiles):
    T, top_k = top_i.shape
    flat_e = top_i.reshape(-1)
    onehot = (flat_e[:, None] == jnp.arange(N_EXPERTS, dtype=jnp.int32)[None, :]).astype(jnp.int32)
    rank = jnp.sum((jnp.cumsum(onehot, axis=0) - onehot) * onehot, axis=1)
    counts = jnp.sum(onehot, axis=0)
    padded = (counts + tm - 1) // tm * tm
    ends = jnp.cumsum(padded)
    starts = ends - padded
    pos = starts[flat_e] + rank
    token = jnp.arange(T * top_k, dtype=jnp.int32) // top_k
    row_src = jnp.zeros((n_tiles * tm,), jnp.int32).at[pos].set(token)
    row_w = jnp.zeros((n_tiles * tm,), F32).at[pos].set(top_w.reshape(-1))
    tile_start = jnp.arange(n_tiles, dtype=jnp.int32) * tm
    tile_expert = jnp.minimum(jnp.sum((tile_start[:, None] >= ends[None, :]).astype(jnp.int32), axis=1),
                              N_EXPERTS - 1)
    n_used = (ends[-1] // tm).reshape(1)
    return pos.astype(jnp.int32), row_src, row_w.reshape(-1, 1), tile_expert.astype(jnp.int32), n_used.astype(jnp.int32)


def _q_up_weight(w):
    r = w.shape[0]
    w = w.reshape(r, MLA_HEADS, MLA_QK)
    nope, rp = w[:, :, :MLA_NOPE], w[:, :, MLA_NOPE:]
    half = MLA_ROPE // 2
    rot = jnp.concatenate([-rp[:, :, half:], rp[:, :, :half]], axis=2)
    z = jnp.zeros((r, MLA_HEADS, V7X_LANES - MLA_ROPE), w.dtype)
    return jnp.concatenate([nope, rp, z, rot, z], axis=2).reshape(r, -1).astype(BF16)


def _kv_up_weight(w):
    r = w.shape[0]
    w = w.reshape(r, MLA_HEADS, MLA_NOPE + MLA_V)
    return jnp.concatenate([w[:, :, :MLA_NOPE].reshape(r, -1), w[:, :, MLA_NOPE:].reshape(r, -1)],
                           axis=1).astype(BF16)


def _rope_tables(positions):
    pos = positions.reshape(-1).astype(F32)[:, None]
    inv_m = ROPE_BASE ** (-jnp.arange(0, MLA_ROPE, 2, dtype=F32) / MLA_ROPE)
    am = pos * inv_m
    zm = jnp.zeros((pos.shape[0], V7X_LANES - MLA_ROPE), F32)
    cos_m = jnp.concatenate([jnp.cos(am), jnp.cos(am), zm], axis=1)
    sin_m = jnp.concatenate([jnp.sin(am), jnp.sin(am), zm], axis=1)
    inv_r = ROPE_BASE ** (-jnp.arange(0, RET_DK, 2, dtype=F32) / RET_DK)
    ar = pos * inv_r
    cos_r = jnp.concatenate([jnp.cos(ar), jnp.cos(ar)], axis=1)
    sin_r = jnp.concatenate([-jnp.sin(ar), jnp.sin(ar)], axis=1)
    return cos_m, sin_m, cos_r, sin_r


def kernel(x, c, positions, w_ada, b_ada, ada_table, norm_g, w_in, mla_q_norm, mla_w_q_up, mla_kv_norm,
           mla_w_kv_up, hg_lb_logits, hg_norm, ret_norm, w_branch, w_out, ffn_w1, ffn_w3, ffn_w2,
           moe_router, moe_w1, moe_w3, moe_w2, final_norm):
    B, S, D = x.shape
    T = B * S
    depth = norm_g.shape[0]
    top_k = 2
    assert depth % 2 == 0, "the final norm is fused into the expert combine of the last (odd) layer"
    assert S % 512 == 0 and D % 512 == 0

    mod = _ada(c, w_ada, b_ada, ada_table)
    cos_m, sin_m, cos_r, sin_r = _rope_tables(positions)

    n_cqkv = MLA_Q_RANK + MLA_KV_RANK
    n_lat = n_cqkv + MLA_ROPE
    n_hg = 4 * HG_WIDTH
    n_ret = 2 * RET_HEADS * RET_DK + 2 * RET_HEADS * RET_DV

    d_in = w_in.shape[2]
    w_in_t = jnp.swapaxes(w_in, 1, 2).reshape(depth, d_in // _WT_GROUP, _WT_GROUP, D)

    xf = x.reshape(T, D)
    pending = None
    out = None
    for l in range(depth):
        shift1, scale1, gate1, shift2, scale2, gate2 = (mod[l, :, j] for j in range(N_MOD))
        if pending is None:
            (h,) = _norm(xf, norm_g[l, 0], scale1, shift1, S)
        else:
            xf, h = _norm(xf, norm_g[l, 0], scale1, shift1, S, resid=pending)
            pending = None

        lat = _matmul(h, w_in_t, l, 0, n_cqkv, transposed=True)
        kr_raw = _matmul(h, w_in_t, l, n_cqkv, V7X_LANES, transposed=True, tn=V7X_LANES)
        hg = _matmul(h, w_in_t, l, n_lat, n_hg, transposed=True)
        rt = _matmul(h, w_in_t, l, n_lat + n_hg, n_ret, transposed=True)
        gates = _matmul(h, w_in_t, l, n_lat + n_hg + n_ret, N_BRANCH * D, transposed=True, mode="sigmoid",
                        out_dtype=BF16)

        y_mla = _mla(lat, kr_raw, cos_m, sin_m, mla_q_norm[l], _q_up_weight(mla_w_q_up[l]),
                     mla_kv_norm[l], _kv_up_weight(mla_w_kv_up[l]), B, S)
        y_hg = _hgrn2(hg, hg_lb_logits, hg_norm[l], l, B, S)
        y_ret = _retention(rt, cos_r, sin_r, ret_norm[l], B, S)

        merged = _merge(y_mla, y_hg, y_ret, w_branch, l, gates)
        xf = _matmul(merged, w_out, l, 0, D, mode="resid", resid=xf, gate=gate1, seq=S)

        if l % 2 == 0:
            (h2,) = _norm(xf, norm_g[l, 1], scale2, shift2, S)
            e = l // 2
            y = _ffn(h2, ffn_w1.astype(BF16), ffn_w3.astype(BF16), ffn_w2.astype(BF16), e, tm=512, tf=256)
            pending = (y, gate2)
        else:
            e = l // 2
            h2, ti, tw = _norm(xf, norm_g[l, 1], scale2, shift2, S, router=moe_router[e], h_dtype=F32)
            tm = 512
            n_tiles = T * top_k // tm + N_EXPERTS
            pos, row_src, row_w, tile_expert, n_used = _route(ti[:, :top_k], tw[:, :top_k], tm, n_tiles)
            xs = _gather_rows(h2, row_src, n_tiles * tm)
            ys = _moe_experts(xs, moe_w1, moe_w3, moe_w2, e, tile_expert, n_used, row_w, tm=tm, tf=512, tn=1024)
            assert l == depth - 1
            out = _combine(pos, xf, gate2, final_norm, ys, S, top_k)
    return out.reshape(B, S, D)
```

```python
import functools
import math

import jax
import jax.numpy as jnp
from jax import lax
from jax.experimental import pallas as pl
from jax.experimental.pallas import tpu as pltpu

F32 = jnp.float32
BF16 = jnp.bfloat16

V7X_LANES = 128
V7X_SUBLANES = 8
V7X_VMEM_BYTES = 64 * 1024 * 1024
V7X_VMEM_LIMIT = V7X_VMEM_BYTES - 8 * 1024 * 1024

EPS = 1e-6
ROPE_BASE = 10000.0
MASK_VALUE = -1e30

MLA_HEADS = 16
MLA_NOPE = 128
MLA_ROPE = 64
MLA_V = 128
MLA_Q_RANK = 1024
MLA_KV_RANK = 512
MLA_QK = MLA_NOPE + MLA_ROPE
MLA_QPAD = 2 * V7X_LANES

HG_HEADS = 16
HG_D = 128
HG_WIDTH = HG_HEADS * HG_D
HG_CHUNK = 64
HG_SUB = 8
HG_MIN_F = 1e-6

RET_HEADS = 8
RET_DK = 128
RET_DV = 256
RET_CHUNK = 128

N_BRANCH = 3
N_EXPERTS = 8
N_MOD = 6


def _params(semantics):
    return pltpu.CompilerParams(dimension_semantics=semantics, vmem_limit_bytes=V7X_VMEM_LIMIT)


def _rms(x, gain):
    return x * lax.rsqrt(jnp.mean(x * x, axis=-1, keepdims=True) + EPS) * gain


def _silu(x):
    return x * jax.nn.sigmoid(x)


def _dot_nt(a, b):
    return lax.dot_general(a, b, (((1,), (1,)), ((), ())), preferred_element_type=F32)


def _dot_tn(a, b):
    return jnp.dot(a.T, b, preferred_element_type=F32)


def _ada_kernel(c_ref, w_ref, b_ref, tab_ref, o_ref, acc_ref):
    k = pl.program_id(0)

    @pl.when(k == 0)
    def _():
        acc_ref[...] = jnp.zeros_like(acc_ref)

    s = _silu(c_ref[...]).astype(BF16)
    acc_ref[...] += jnp.dot(s, w_ref[...].astype(BF16), preferred_element_type=F32)

    @pl.when(k == pl.num_programs(0) - 1)
    def _():
        acc = acc_ref[...] + b_ref[...]
        for l in range(o_ref.shape[0]):
            o_ref[l] = acc + tab_ref[l]


def _ada(c, w_ada, b_ada, ada_table):
    B, D = c.shape
    depth = ada_table.shape[0]
    N = w_ada.shape[1]
    tk = V7X_LANES
    cp = jnp.zeros((V7X_SUBLANES, D), F32).at[:B].set(c)
    out = pl.pallas_call(
        _ada_kernel,
        out_shape=jax.ShapeDtypeStruct((depth, V7X_SUBLANES, N), F32),
        grid=(D // tk,),
        in_specs=[
            pl.BlockSpec((V7X_SUBLANES, tk), lambda k: (0, k)),
            pl.BlockSpec((tk, N), lambda k: (k, 0)),
            pl.BlockSpec((1, N), lambda k: (0, 0)),
            pl.BlockSpec((depth, 1, N), lambda k: (0, 0, 0)),
        ],
        out_specs=pl.BlockSpec((depth, V7X_SUBLANES, N), lambda k: (0, 0, 0)),
        scratch_shapes=[pltpu.VMEM((V7X_SUBLANES, N), F32)],
        compiler_params=_params(("arbitrary",)),
        name="ada_mod",
    )(cp, w_ada, b_ada.reshape(1, N), ada_table.reshape(depth, 1, N))
    return out[:, :B].reshape(depth, B, N_MOD, D)


def _norm_kernel(*refs, has_resid, has_router, n_experts):
    it = iter(refs)
    x_ref = next(it)
    if has_resid:
        y_ref, gt_ref = next(it), next(it)
    g_ref, sc_ref, sh_ref = next(it), next(it), next(it)
    if has_router:
        r_ref = next(it)
    if has_resid:
        xo_ref = next(it)
    h_ref = next(it)
    if has_router:
        ti_ref, tw_ref = next(it), next(it)

    x = x_ref[...]
    if has_resid:
        x = x + gt_ref[0] * y_ref[...]
        xo_ref[...] = x
    h = _rms(x, g_ref[...]) * (1.0 + sc_ref[0]) + sh_ref[0]
    if not has_router:
        h_ref[...] = h.astype(h_ref.dtype)
    else:
        h_ref[...] = _pack_bf16_pairs(h)
        logits = jnp.dot(h, r_ref[...], preferred_element_type=F32, precision=lax.Precision.HIGHEST)
        lane = lax.broadcasted_iota(jnp.int32, logits.shape, 1).astype(F32)
        neg = jnp.float32(-jnp.inf)
        big = jnp.float32(V7X_LANES)
        lg = jnp.where(lane < n_experts, logits, neg)
        m1 = jnp.max(lg, axis=-1, keepdims=True)
        i1 = jnp.min(jnp.where(lg == m1, lane, big), axis=-1, keepdims=True)
        lg2 = jnp.where(lane == i1, neg, lg)
        m2 = jnp.max(lg2, axis=-1, keepdims=True)
        i2 = jnp.min(jnp.where(lg2 == m2, lane, big), axis=-1, keepdims=True)
        e = jnp.exp(m2 - m1)
        w1 = 1.0 / (1.0 + e)
        w2 = e / (1.0 + e)
        ti_ref[...] = jnp.where(lane == 0.0, i1, jnp.where(lane == 1.0, i2, 0.0)).astype(jnp.int32)
        tw_ref[...] = jnp.where(lane == 0.0, w1, jnp.where(lane == 1.0, w2, 0.0))


def _norm(x, gain, scale, shift, seq, *, resid=None, router=None):
    T, D = x.shape
    B = scale.shape[0]
    tm = 256
    row = pl.BlockSpec((tm, D), lambda i: (i, 0))
    per_batch = pl.BlockSpec((1, 1, D), lambda i: (i * tm // seq, 0, 0))
    args, in_specs = [x], [row]
    if resid is not None:
        args += [resid[0], resid[1].reshape(B, 1, D)]
        in_specs += [row, per_batch]
    args += [gain.reshape(1, D), scale.reshape(B, 1, D), shift.reshape(B, 1, D)]
    in_specs += [pl.BlockSpec((1, D), lambda i: (0, 0)), per_batch, per_batch]
    out_shape, out_specs = [], []
    if router is not None:
        n_experts = router.shape[1]
        rp = jnp.zeros((D, V7X_LANES), F32).at[:, :n_experts].set(router)
        args.append(rp)
        in_specs.append(pl.BlockSpec((D, V7X_LANES), lambda i: (0, 0)))
    else:
        n_experts = 0
    if resid is not None:
        out_shape.append(jax.ShapeDtypeStruct((T, D), F32))
        out_specs.append(row)
    if router is None:
        out_shape.append(jax.ShapeDtypeStruct((T, D), BF16))
        out_specs.append(row)
    else:
        out_shape.append(jax.ShapeDtypeStruct((T, D // 2), jnp.uint32))
        out_specs.append(pl.BlockSpec((tm, D // 2), lambda i: (i, 0)))
    if router is not None:
        small = pl.BlockSpec((tm, V7X_LANES), lambda i: (i, 0))
        out_shape += [jax.ShapeDtypeStruct((T, V7X_LANES), jnp.int32), jax.ShapeDtypeStruct((T, V7X_LANES), F32)]
        out_specs += [small, small]
    return pl.pallas_call(
        functools.partial(_norm_kernel, has_resid=resid is not None, has_router=router is not None,
                          n_experts=n_experts),
        out_shape=out_shape,
        grid=(T // tm,),
        in_specs=in_specs,
        out_specs=out_specs,
        compiler_params=_params(("arbitrary",)),
        name="ada_norm",
    )(*args)


_W_CAST_ROWS = 512


_WT_GROUP = 64


def _mm_kernel(*refs, mode, n_groups):
    it = iter(refs)
    a_ref = next(it)
    w_refs = [next(it) for _ in range(max(n_groups, 1))]
    if mode == "resid":
        r_ref, g_ref = next(it), next(it)
    o_ref, wscr = next(it), next(it)
    c = pl.program_id(2)

    @pl.when(pl.program_id(1) == 0)
    def _():
        if n_groups:
            for g, w_ref in enumerate(w_refs):
                wscr[c, g * _WT_GROUP:(g + 1) * _WT_GROUP, :] = w_ref[0, 0].astype(BF16)
        else:
            for r0 in range(0, wscr.shape[1], _W_CAST_ROWS):
                wscr[c, r0:r0 + _W_CAST_ROWS, :] = w_refs[0][0, r0:r0 + _W_CAST_ROWS, :].astype(BF16)

    if n_groups:
        acc = _dot_nt(a_ref[...], wscr[c])
    else:
        acc = jnp.dot(a_ref[...], wscr[c], preferred_element_type=F32)
    if mode == "plain":
        o_ref[...] = acc.astype(o_ref.dtype)
    elif mode == "sigmoid":
        o_ref[...] = jax.nn.sigmoid(acc).astype(o_ref.dtype)
    else:
        o_ref[...] = r_ref[...] + g_ref[0] * acc


def _matmul(a, w, layer, col0, n, *, transposed=False, mode="plain", out_dtype=F32, resid=None, gate=None,
            seq=None, tm=1024, tn=512):
    M, K = a.shape
    assert n % tn == 0 and M % tm == 0
    pair = 2 if (n // tn) % 2 == 0 else 1

    def col_tile(jj, i, c):
        return jj * pair + jnp.where(i == 0, c, pair - 1)

    args = [a]
    in_specs = [pl.BlockSpec((tm, K), lambda jj, i, c: (i, 0))]
    if transposed:
        assert col0 % _WT_GROUP == 0 and tn % _WT_GROUP == 0
        n_groups = tn // _WT_GROUP
        g0 = col0 // _WT_GROUP
        for g in range(n_groups):
            args.append(w)
            in_specs.append(pl.BlockSpec(
                (1, 1, _WT_GROUP, K),
                lambda jj, i, c, g=g: (layer, g0 + col_tile(jj, i, c) * n_groups + g, 0, 0)))
        wscr = pltpu.VMEM((pair, tn, K), BF16)
    else:
        assert col0 % tn == 0 and K % _W_CAST_ROWS == 0
        n_groups = 0
        args.append(w)
        in_specs.append(pl.BlockSpec((1, K, tn), lambda jj, i, c: (layer, 0, col0 // tn + col_tile(jj, i, c))))
        wscr = pltpu.VMEM((pair, K, tn), BF16)
    out_spec = pl.BlockSpec((tm, tn), lambda jj, i, c: (i, jj * pair + c))
    if mode == "resid":
        B = gate.shape[0]
        args += [resid, gate.reshape(B, 1, n)]
        in_specs += [out_spec, pl.BlockSpec((1, 1, tn), lambda jj, i, c: (i * tm // seq, 0, jj * pair + c))]
    return pl.pallas_call(
        functools.partial(_mm_kernel, mode=mode, n_groups=n_groups),
        out_shape=jax.ShapeDtypeStruct((M, n), out_dtype),
        grid=(n // (tn * pair), M // tm, pair),
        in_specs=in_specs,
        out_specs=out_spec,
        scratch_shapes=[wscr],
        compiler_params=_params(("arbitrary", "arbitrary", "arbitrary")),
        name="matmul_" + mode,
    )(*args)


def _mla_q_kernel(cq_ref, g_ref, w_ref, cos_ref, sin_ref, o_ref, a_scr, *, heads_per_step, scale):
    @pl.when(pl.program_id(1) == 0)
    def _():
        a_scr[...] = _rms(cq_ref[...], g_ref[...]).astype(BF16)

    acc = jnp.dot(a_scr[...], w_ref[...], preferred_element_type=F32)
    cos, sin = cos_ref[...], sin_ref[...]
    L = V7X_LANES
    for h in range(heads_per_step):
        nope = acc[:, 3 * L * h:3 * L * h + L]
        rp = acc[:, 3 * L * h + L:3 * L * h + 2 * L]
        rt = acc[:, 3 * L * h + 2 * L:3 * L * h + 3 * L]
        o_ref[:, 2 * L * h:2 * L * h + L] = (nope * scale).astype(BF16)
        o_ref[:, 2 * L * h + L:2 * L * h + 2 * L] = ((rp * cos + rt * sin) * scale).astype(BF16)


def _mla_kv_kernel(ckv_ref, g_ref, w_ref, kraw_ref, cos_ref, sin_ref, kv_ref, kr_ref, a_scr):
    @pl.when(pl.program_id(1) == 0)
    def _():
        a_scr[...] = _rms(ckv_ref[...], g_ref[...]).astype(BF16)
        x = kraw_ref[...]
        half = MLA_ROPE // 2
        lane = lax.broadcasted_iota(jnp.int32, x.shape, 1)
        swapped = jnp.where(lane < half, -pltpu.roll(x, V7X_LANES - half, 1), pltpu.roll(x, half, 1))
        roped = x * cos_ref[...] + swapped * sin_ref[...]
        kr_ref[...] = jnp.where(lane < MLA_ROPE, roped, 0.0).astype(BF16)

    kv_ref[...] = jnp.dot(a_scr[...], w_ref[...], preferred_element_type=F32).astype(BF16)


def _mla_attn_kernel(q_ref, k_ref, kr_ref, v_ref, o_ref, *, tq, n_heads):
    i = pl.program_id(2)
    qs = [q_ref[:, h * MLA_QPAD:(h + 1) * MLA_QPAD] for h in range(n_heads)]

    def block(j, carry, masked):
        off = pl.multiple_of(j * tq, tq)
        kr = kr_ref[pl.ds(off, tq), :]
        out = []
        for h in range(n_heads):
            m, l, acc = carry[h]
            kk = jnp.concatenate([k_ref[pl.ds(off, tq), h * MLA_NOPE:(h + 1) * MLA_NOPE], kr], axis=1)
            s = _dot_nt(qs[h], kk)
            if masked:
                r = lax.broadcasted_iota(jnp.int32, s.shape, 0)
                c = lax.broadcasted_iota(jnp.int32, s.shape, 1)
                s = jnp.where(c <= r, s, MASK_VALUE)
            m_new = jnp.maximum(m, jnp.max(s, axis=-1, keepdims=True))
            alpha = jnp.exp(m - m_new)
            p = jnp.exp(s - m_new)
            l = alpha * l + jnp.sum(p, axis=-1, keepdims=True)
            v = v_ref[pl.ds(off, tq), h * MLA_V:(h + 1) * MLA_V]
            acc = alpha * acc + jnp.dot(p.astype(BF16), v, preferred_element_type=F32)
            out.append((m_new, l, acc))
        return tuple(out)

    carry = tuple((jnp.full((tq, 1), MASK_VALUE, F32), jnp.zeros((tq, 1), F32), jnp.zeros((tq, MLA_V), F32))
                  for _ in range(n_heads))
    carry = lax.fori_loop(0, i, lambda j, c: block(j, c, False), carry)
    carry = block(i, carry, True)
    for h in range(n_heads):
        _, l, acc = carry[h]
        o_ref[:, h * MLA_V:(h + 1) * MLA_V] = (acc / l).astype(o_ref.dtype)


def _mla(lat, kr_raw, cos_m, sin_m, q_norm, wq, kv_norm, wkv, B, S):
    T = lat.shape[0]
    L = V7X_LANES
    tm = 512
    hps = 4
    scale = MLA_QK ** -0.5
    q = pl.pallas_call(
        functools.partial(_mla_q_kernel, heads_per_step=hps, scale=scale),
        out_shape=jax.ShapeDtypeStruct((T, MLA_HEADS * MLA_QPAD), BF16),
        grid=(T // tm, MLA_HEADS // hps),
        in_specs=[
            pl.BlockSpec((tm, MLA_Q_RANK), lambda i, j: (i, 0)),
            pl.BlockSpec((1, MLA_Q_RANK), lambda i, j: (0, 0)),
            pl.BlockSpec((MLA_Q_RANK, hps * 3 * L), lambda i, j: (0, j)),
            pl.BlockSpec((tm, L), lambda i, j: (i, 0)),
            pl.BlockSpec((tm, L), lambda i, j: (i, 0)),
        ],
        out_specs=pl.BlockSpec((tm, hps * MLA_QPAD), lambda i, j: (i, j)),
        scratch_shapes=[pltpu.VMEM((tm, MLA_Q_RANK), BF16)],
        compiler_params=_params(("arbitrary", "arbitrary")),
        name="mla_q",
    )(lat, q_norm.reshape(1, -1), wq, cos_m, sin_m)

    tn = 1024
    kv_w = 2 * MLA_HEADS * MLA_NOPE
    ckv_blk = MLA_Q_RANK // MLA_KV_RANK
    kv, kr = pl.pallas_call(
        _mla_kv_kernel,
        out_shape=[jax.ShapeDtypeStruct((T, kv_w), BF16), jax.ShapeDtypeStruct((T, L), BF16)],
        grid=(T // tm, kv_w // tn),
        in_specs=[
            pl.BlockSpec((tm, MLA_KV_RANK), lambda i, j: (i, ckv_blk)),
            pl.BlockSpec((1, MLA_KV_RANK), lambda i, j: (0, 0)),
            pl.BlockSpec((MLA_KV_RANK, tn), lambda i, j: (0, j)),
            pl.BlockSpec((tm, L), lambda i, j: (i, 0)),
            pl.BlockSpec((tm, L), lambda i, j: (i, 0)),
            pl.BlockSpec((tm, L), lambda i, j: (i, 0)),
        ],
        out_specs=[pl.BlockSpec((tm, tn), lambda i, j: (i, j)), pl.BlockSpec((tm, L), lambda i, j: (i, 0))],
        scratch_shapes=[pltpu.VMEM((tm, MLA_KV_RANK), BF16)],
        compiler_params=_params(("arbitrary", "arbitrary")),
        name="mla_kv",
    )(lat, kv_norm.reshape(1, -1), wkv, kr_raw, cos_m, sin_m)

    tq = 512
    nq = S // tq
    nh = 4
    hg = MLA_HEADS // nh
    return pl.pallas_call(
        functools.partial(_mla_attn_kernel, tq=tq, n_heads=nh),
        out_shape=jax.ShapeDtypeStruct((T, MLA_HEADS * MLA_V), BF16),
        grid=(B, hg, nq),
        in_specs=[
            pl.BlockSpec((tq, nh * MLA_QPAD), lambda b, h, i: (b * nq + i, h)),
            pl.BlockSpec((S, nh * MLA_NOPE), lambda b, h, i: (b, h)),
            pl.BlockSpec((S, L), lambda b, h, i: (b, 0)),
            pl.BlockSpec((S, nh * MLA_V), lambda b, h, i: (b, hg + h)),
        ],
        out_specs=pl.BlockSpec((tq, nh * MLA_V), lambda b, h, i: (b * nq + i, h)),
        compiler_params=_params(("arbitrary", "arbitrary", "arbitrary")),
        name="mla_attn",
    )(q, kv, kr, kv)


def _hgrn2_kernel(q_ref, f_ref, i_ref, gate_ref, lbl_ref, ng_ref, o_ref, st_scr, q_scr, k_scr, b_scr, *, layer):
    tb = q_ref.shape[0]
    C = HG_CHUNK

    @pl.when(pl.program_id(2) == 0)
    def _():
        st_scr[...] = jnp.zeros_like(st_scr)

    lg = lbl_ref[...]
    e = jnp.exp(lg - jnp.max(lg, axis=0, keepdims=True))
    p = e / jnp.sum(e, axis=0, keepdims=True)
    lb = jnp.zeros((1, HG_D), F32)
    for l in range(1, layer + 1):
        lb = lb + p[l:l + 1, :]

    fr = f_ref[...]
    f = lb + (1.0 - lb) * jax.nn.sigmoid(fr)
    logf = jnp.log(jnp.maximum(f, HG_MIN_F))
    q_scr[...] = _silu(q_ref[...])
    k_scr[...] = (1.0 - lb) * jax.nn.sigmoid(-fr)

    hi = logf.astype(BF16)
    r1 = logf - hi.astype(F32)
    mid = r1.astype(BF16)
    lo = (r1 - mid.astype(F32)).astype(BF16)
    rr = lax.broadcasted_iota(jnp.int32, (C, C), 0)
    cc = lax.broadcasted_iota(jnp.int32, (C, C), 1)
    causal = cc <= rr
    tril = jnp.where(causal, 1.0, 0.0).astype(BF16)
    for c in range(tb // C):
        sl = slice(c * C, (c + 1) * C)
        parts = jnp.concatenate([hi[sl], mid[sl], lo[sl]], axis=1)
        cs = jnp.dot(tril, parts, preferred_element_type=F32)
        b_scr[sl, :] = cs[:, :HG_D] + cs[:, HG_D:2 * HG_D] + cs[:, 2 * HG_D:]

    st = st_scr[...]
    ng = ng_ref[...]
    for c in range(tb // C):
        r0 = c * C
        b_c = b_scr[r0:r0 + C, :]
        q_c = q_scr[r0:r0 + C, :]
        k_c = k_scr[r0:r0 + C, :]
        v_c = i_ref[r0:r0 + C, :]
        b_last = b_scr[r0 + C - 1:r0 + C, :]
        rows = []
        for blk in range(C // HG_SUB):
            s0 = r0 + blk * HG_SUB
            n = (blk + 1) * HG_SUB
            ref = b_scr[s0 + HG_SUB // 2 - 1:s0 + HG_SUB // 2, :]
            q_b = q_scr[s0:s0 + HG_SUB, :] * jnp.exp(b_scr[s0:s0 + HG_SUB, :] - ref)
            k_b = k_scr[r0:r0 + n, :] * jnp.exp(ref - b_scr[r0:r0 + n, :])
            if n < C:
                k_b = jnp.concatenate([k_b, jnp.zeros((C - n, HG_D), F32)], axis=0)
            rows.append(_dot_nt(q_b, k_b))
        scores = jnp.where(causal, jnp.concatenate(rows, axis=0), 0.0)
        o = jnp.dot(scores, v_c, preferred_element_type=F32) + _dot_nt(q_c * jnp.exp(b_c), st)
        st = st * jnp.exp(b_last) + _dot_tn(v_c, k_c * jnp.exp(b_last - b_c))
        y = _rms(o, ng) * _silu(gate_ref[r0:r0 + C, :])
        o_ref[r0:r0 + C, :] = y.astype(o_ref.dtype)
    st_scr[...] = st


def _hgrn2(hg, lb_logits, norm_g, layer, B, S):
    T = hg.shape[0]
    tb = 512
    nb = S // tb
    depth = lb_logits.shape[0]

    def part(p):
        return pl.BlockSpec((tb, HG_D), lambda b, h, j: (b * nb + j, p * HG_HEADS + h))

    return pl.pallas_call(
        functools.partial(_hgrn2_kernel, layer=layer),
        out_shape=jax.ShapeDtypeStruct((T, HG_WIDTH), BF16),
        grid=(B, HG_HEADS, nb),
        in_specs=[part(0), part(1), part(2), part(3),
                  pl.BlockSpec((depth, HG_D), lambda b, h, j: (0, h)),
                  pl.BlockSpec((1, HG_D), lambda b, h, j: (0, h))],
        out_specs=pl.BlockSpec((tb, HG_D), lambda b, h, j: (b * nb + j, h)),
        scratch_shapes=[pltpu.VMEM((HG_D, HG_D), F32), pltpu.VMEM((tb, HG_D), F32),
                        pltpu.VMEM((tb, HG_D), F32), pltpu.VMEM((tb, HG_D), F32)],
        compiler_params=_params(("arbitrary", "arbitrary", "arbitrary")),
        name="hgrn2",
    )(hg, hg, hg, hg, lb_logits, norm_g.reshape(1, -1))


def _ret_kernel(q_ref, k_ref, v_ref, gate_ref, cos_ref, sin_ref, lg_ref, ng_ref, o_ref, st_scr):
    tb = q_ref.shape[0]
    C = RET_CHUNK

    @pl.when(pl.program_id(2) == 0)
    def _():
        st_scr[...] = jnp.zeros_like(st_scr)

    cos, sin = cos_ref[...], sin_ref[...]
    half = RET_DK // 2
    xq, xk = q_ref[...], k_ref[...]
    q = xq * cos + pltpu.roll(xq, half, 1) * sin
    k = (xk * cos + pltpu.roll(xk, half, 1) * sin) * (RET_DK ** -0.5)

    lg = lg_ref[0]
    rr = lax.broadcasted_iota(jnp.int32, (C, C), 0)
    cc = lax.broadcasted_iota(jnp.int32, (C, C), 1)
    rel = (rr - cc).astype(F32)
    decay = jnp.where(rel >= 0.0, jnp.exp(lg * jnp.maximum(rel, 0.0)), 0.0)
    idx = rr.astype(F32)
    q_dec = jnp.exp(lg * (idx + 1.0))
    k_dec = jnp.exp(lg * (C - 1.0 - idx))
    chunk_dec = jnp.exp(lg * float(C))

    st = st_scr[...]
    ng = ng_ref[...]
    for c in range(tb // C):
        sl = slice(c * C, (c + 1) * C)
        q_c, k_c = q[sl], k[sl]
        v_c = v_ref[sl, :]
        s = _dot_nt(q_c.astype(BF16), k_c.astype(BF16)) * decay
        o = jnp.dot(s.astype(BF16), v_c.astype(BF16), preferred_element_type=F32)
        o = o + _dot_nt((q_c * q_dec).astype(BF16), st.astype(BF16))
        st = st * chunk_dec + jnp.dot(v_c.T.astype(BF16), (k_c * k_dec).astype(BF16),
                                      preferred_element_type=F32)
        y = _rms(o, ng) * _silu(gate_ref[sl, :])
        o_ref[sl, :] = y.astype(o_ref.dtype)
    st_scr[...] = st


def _retention(rt, cos_r, sin_r, norm_g, B, S):
    T = rt.shape[0]
    tb = 512
    nb = S // tb
    H = RET_HEADS
    v_blk0 = 2 * H * RET_DK // RET_DV
    log_gamma = jnp.log1p(-(2.0 ** (-5.0 - jnp.arange(H, dtype=F32))))
    lg = jnp.broadcast_to(log_gamma[:, None, None], (H, 1, RET_CHUNK))
    return pl.pallas_call(
        _ret_kernel,
        out_shape=jax.ShapeDtypeStruct((T, H * RET_DV), BF16),
        grid=(B, H, nb),
        in_specs=[
            pl.BlockSpec((tb, RET_DK), lambda b, h, j: (b * nb + j, h)),
            pl.BlockSpec((tb, RET_DK), lambda b, h, j: (b * nb + j, H + h)),
            pl.BlockSpec((tb, RET_DV), lambda b, h, j: (b * nb + j, v_blk0 + h)),
            pl.BlockSpec((tb, RET_DV), lambda b, h, j: (b * nb + j, v_blk0 + H + h)),
            pl.BlockSpec((tb, RET_DK), lambda b, h, j: (b * nb + j, 0)),
            pl.BlockSpec((tb, RET_DK), lambda b, h, j: (b * nb + j, 0)),
            pl.BlockSpec((1, 1, RET_CHUNK), lambda b, h, j: (h, 0, 0)),
            pl.BlockSpec((1, RET_DV), lambda b, h, j: (0, h)),
        ],
        out_specs=pl.BlockSpec((tb, RET_DV), lambda b, h, j: (b * nb + j, h)),
        scratch_shapes=[pltpu.VMEM((RET_DV, RET_DK), F32)],
        compiler_params=_params(("arbitrary", "arbitrary", "arbitrary")),
        name="retention",
    )(rt, rt, rt, rt, cos_r, sin_r, lg, norm_g.reshape(1, -1))


def _merge_kernel(ya_ref, yb_ref, yc_ref, w_ref, ga_ref, gb_ref, gc_ref, o_ref, wscr):
    @pl.when(pl.program_id(1) == 0)
    def _():
        for br in range(N_BRANCH):
            for r0 in range(0, wscr.shape[1], _W_CAST_ROWS):
                wscr[br, r0:r0 + _W_CAST_ROWS, :] = w_ref[0, br, r0:r0 + _W_CAST_ROWS, :].astype(BF16)

    acc = ga_ref[...].astype(F32) * jnp.dot(ya_ref[...], wscr[0], preferred_element_type=F32)
    acc = acc + gb_ref[...].astype(F32) * jnp.dot(yb_ref[...], wscr[1], preferred_element_type=F32)
    acc = acc + gc_ref[...].astype(F32) * jnp.dot(yc_ref[...], wscr[2], preferred_element_type=F32)
    o_ref[...] = acc.astype(o_ref.dtype)


def _merge(ya, yb, yc, w_branch, layer, gates):
    T, K = ya.shape
    D = w_branch.shape[3]
    tm, tn = 512, 512
    nj = D // tn
    ybs = pl.BlockSpec((tm, K), lambda j, i: (i, 0))
    return pl.pallas_call(
        _merge_kernel,
        out_shape=jax.ShapeDtypeStruct((T, D), BF16),
        grid=(nj, T // tm),
        in_specs=[ybs, ybs, ybs,
                  pl.BlockSpec((1, N_BRANCH, K, tn), lambda j, i: (layer, 0, 0, j)),
                  pl.BlockSpec((tm, tn), lambda j, i: (i, j)),
                  pl.BlockSpec((tm, tn), lambda j, i: (i, nj + j)),
                  pl.BlockSpec((tm, tn), lambda j, i: (i, 2 * nj + j))],
        out_specs=pl.BlockSpec((tm, tn), lambda j, i: (i, j)),
        scratch_shapes=[pltpu.VMEM((N_BRANCH, K, tn), BF16)],
        compiler_params=_params(("arbitrary", "arbitrary")),
        name="merge",
    )(ya, yb, yc, w_branch, gates, gates, gates)


def _ffn_kernel(x_ref, w1_ref, w3_ref, w2_ref, o_ref):
    @pl.when(pl.program_id(1) == 0)
    def _():
        o_ref[...] = jnp.zeros_like(o_ref)

    x = x_ref[...]
    g = jnp.dot(x, w1_ref[0], preferred_element_type=F32)
    u = jnp.dot(x, w3_ref[0], preferred_element_type=F32)
    o_ref[...] += jnp.dot((_silu(g) * u).astype(BF16), w2_ref[0], preferred_element_type=F32)


def _ffn(x, w1, w3, w2, e, *, tm, tf):
    M, D = x.shape
    F = w1.shape[2]
    assert M % tm == 0 and F % tf == 0
    w13 = pl.BlockSpec((1, D, tf), lambda i, f: (e, 0, f))
    return pl.pallas_call(
        _ffn_kernel,
        out_shape=jax.ShapeDtypeStruct((M, D), F32),
        grid=(M // tm, F // tf),
        in_specs=[pl.BlockSpec((tm, D), lambda i, f: (i, 0)), w13, w13,
                  pl.BlockSpec((1, tf, D), lambda i, f: (e, f, 0))],
        out_specs=pl.BlockSpec((tm, D), lambda i, f: (i, 0)),
        compiler_params=_params(("arbitrary", "arbitrary")),
        name="ffn_dense",
    )(x, w1, w3, w2)


def _expert_changed(te_ref, i):
    return jnp.logical_or(i == 0, te_ref[i] != te_ref[jnp.maximum(i - 1, 0)])


def _cast_rows(src_ref, dst_ref):
    for r0 in range(0, dst_ref.shape[0], _W_CAST_ROWS):
        dst_ref[r0:r0 + _W_CAST_ROWS, :] = src_ref[0, 0, r0:r0 + _W_CAST_ROWS, :].astype(BF16)


def _moe_up_kernel(te_ref, nu_ref, x_ref, w1_ref, w3_ref, o_ref, w1s, w3s):
    i = pl.program_id(1)

    @pl.when(_expert_changed(te_ref, i))
    def _():
        _cast_rows(w1_ref, w1s)
        _cast_rows(w3_ref, w3s)

    @pl.when(i < nu_ref[0])
    def _():
        x = x_ref[...]
        g = jnp.dot(x, w1s[...], preferred_element_type=F32)
        u = jnp.dot(x, w3s[...], preferred_element_type=F32)
        o_ref[...] = (_silu(g) * u).astype(o_ref.dtype)

    @pl.when(i >= nu_ref[0])
    def _():
        o_ref[...] = jnp.zeros_like(o_ref)


def _moe_down_kernel(te_ref, nu_ref, h_ref, w2_ref, rw_ref, o_ref, w2s):
    i = pl.program_id(1)

    @pl.when(_expert_changed(te_ref, i))
    def _():
        _cast_rows(w2_ref, w2s)

    @pl.when(i < nu_ref[0])
    def _():
        o_ref[...] = jnp.dot(h_ref[...], w2s[...], preferred_element_type=F32) * rw_ref[...]

    @pl.when(i >= nu_ref[0])
    def _():
        o_ref[...] = jnp.zeros_like(o_ref)


def _moe_experts(xs, w1, w3, w2, e, tile_expert, n_used, row_w, *, tm, tf, tn):
    P, D = xs.shape
    F = w1.shape[3]
    assert P % tm == 0 and F % tf == 0 and D % tn == 0 and D % _W_CAST_ROWS == 0 and F % _W_CAST_ROWS == 0
    n_tiles = P // tm
    hmid = pl.pallas_call(
        _moe_up_kernel,
        out_shape=jax.ShapeDtypeStruct((P, F), BF16),
        grid_spec=pltpu.PrefetchScalarGridSpec(
            num_scalar_prefetch=2, grid=(F // tf, n_tiles),
            in_specs=[pl.BlockSpec((tm, D), lambda f, i, te, nu: (i, 0)),
                      pl.BlockSpec((1, 1, D, tf), lambda f, i, te, nu: (e, te[i], 0, f)),
                      pl.BlockSpec((1, 1, D, tf), lambda f, i, te, nu: (e, te[i], 0, f))],
            out_specs=pl.BlockSpec((tm, tf), lambda f, i, te, nu: (i, f)),
            scratch_shapes=[pltpu.VMEM((D, tf), BF16), pltpu.VMEM((D, tf), BF16)]),
        compiler_params=_params(("arbitrary", "arbitrary")),
        name="moe_up",
    )(tile_expert, n_used, xs, w1, w3)
    return pl.pallas_call(
        _moe_down_kernel,
        out_shape=jax.ShapeDtypeStruct((P, D), F32),
        grid_spec=pltpu.PrefetchScalarGridSpec(
            num_scalar_prefetch=2, grid=(D // tn, n_tiles),
            in_specs=[pl.BlockSpec((tm, F), lambda n, i, te, nu: (i, 0)),
                      pl.BlockSpec((1, 1, F, tn), lambda n, i, te, nu: (e, te[i], 0, n)),
                      pl.BlockSpec((tm, 1), lambda n, i, te, nu: (i, 0))],
            out_specs=pl.BlockSpec((tm, tn), lambda n, i, te, nu: (i, n)),
            scratch_shapes=[pltpu.VMEM((F, tn), BF16)]),
        compiler_params=_params(("arbitrary", "arbitrary")),
        name="moe_down",
    )(tile_expert, n_used, hmid, w2, row_w)


def _pack_bf16_pairs(h):
    w = h.shape[1] // 2
    hb = h.astype(BF16).astype(F32)
    hi = lax.bitcast_convert_type(hb[:, :w], jnp.uint32)
    lo = lax.bitcast_convert_type(hb[:, w:], jnp.uint32)
    return hi | (lo >> 16)


def _unpack_bf16_pairs(u):
    hi = lax.bitcast_convert_type(u & jnp.uint32(0xFFFF0000), F32).astype(BF16)
    lo = lax.bitcast_convert_type(u << 16, F32).astype(BF16)
    return hi, lo


_DMA_UNROLL = 8


def _wait_rows(src_hbm, dst, sem):
    pltpu.make_async_copy(src_hbm.at[pl.ds(0, dst.shape[0]), :], dst, sem).wait()


def _gather_kernel(src_ref, h_hbm, o_ref, buf, sem, *, rows):
    i = pl.program_id(0)
    n = pl.num_programs(0)

    def issue(tile):
        slot = tile % 2

        def body(r, carry):
            pltpu.make_async_copy(h_hbm.at[pl.ds(src_ref[tile * rows + r], 1), :],
                                  buf.at[slot, pl.ds(r, 1), :], sem.at[slot]).start()
            return carry

        lax.fori_loop(0, rows, body, 0, unroll=_DMA_UNROLL)

    @pl.when(i == 0)
    def _():
        issue(i)

    @pl.when(i + 1 < n)
    def _():
        issue(i + 1)

    slot = i % 2
    _wait_rows(h_hbm, buf.at[slot], sem.at[slot])
    w = buf.shape[2]
    hi, lo = _unpack_bf16_pairs(buf[slot])
    o_ref[:, :w] = hi
    o_ref[:, w:] = lo


def _gather_rows(h_packed, row_src, n_rows):
    T, W = h_packed.shape
    rows = 256
    return pl.pallas_call(
        functools.partial(_gather_kernel, rows=rows),
        out_shape=jax.ShapeDtypeStruct((n_rows, 2 * W), BF16),
        grid_spec=pltpu.PrefetchScalarGridSpec(
            num_scalar_prefetch=1, grid=(n_rows // rows,),
            in_specs=[pl.BlockSpec(memory_space=pl.ANY)],
            out_specs=pl.BlockSpec((rows, 2 * W), lambda i, src: (i, 0)),
            scratch_shapes=[pltpu.VMEM((2, rows, W), jnp.uint32), pltpu.SemaphoreType.DMA((2,))]),
        compiler_params=_params(("arbitrary",)),
        name="moe_gather",
    )(row_src, h_packed)


def _combine_kernel(pos_ref, x_ref, gt_ref, fg_ref, ys_hbm, o_ref, buf, sem, *, rows, top_k):
    i = pl.program_id(0)
    n = pl.num_programs(0)

    def issue(tile):
        slot = tile % 2

        def body(r, carry):
            for s in range(top_k):
                pltpu.make_async_copy(ys_hbm.at[pl.ds(pos_ref[(tile * rows + r) * top_k + s], 1), :],
                                      buf.at[slot, s, pl.ds(r, 1), :], sem.at[slot]).start()
            return carry

        lax.fori_loop(0, rows, body, 0, unroll=_DMA_UNROLL)

    @pl.when(i == 0)
    def _():
        issue(i)

    @pl.when(i + 1 < n)
    def _():
        issue(i + 1)

    slot = i % 2
    for s in range(top_k):
        _wait_rows(ys_hbm, buf.at[slot, s], sem.at[slot])
    y = buf[slot, 0]
    for s in range(1, top_k):
        y = y + buf[slot, s]
    x = x_ref[...] + gt_ref[0] * y
    o_ref[...] = _rms(x, fg_ref[...])


def _combine(pos, x, gate, final_g, ys, seq, top_k):
    T, D = x.shape
    B = gate.shape[0]
    rows = 256
    return pl.pallas_call(
        functools.partial(_combine_kernel, rows=rows, top_k=top_k),
        out_shape=jax.ShapeDtypeStruct((T, D), F32),
        grid_spec=pltpu.PrefetchScalarGridSpec(
            num_scalar_prefetch=1, grid=(T // rows,),
            in_specs=[pl.BlockSpec((rows, D), lambda i, p: (i, 0)),
                      pl.BlockSpec((1, 1, D), lambda i, p: (i * rows // seq, 0, 0)),
                      pl.BlockSpec((1, D), lambda i, p: (0, 0)),
                      pl.BlockSpec(memory_space=pl.ANY)],
            out_specs=pl.BlockSpec((rows, D), lambda i, p: (i, 0)),
            scratch_shapes=[pltpu.VMEM((2, top_k, rows, D), F32), pltpu.SemaphoreType.DMA((2,))]),
        compiler_params=_params(("arbitrary",)),
        name="moe_combine",
    )(pos, x, gate.reshape(B, 1, D), final_g.reshape(1, D), ys)


def _route(top_i, top_w, tm, n_tiles):
    T, top_k = top_i.shape
    flat_e = top_i.reshape(-1)
    onehot = (flat_e[:, None] == jnp.arange(N_EXPERTS, dtype=jnp.int32)[None, :]).astype(jnp.int32)
    rank = jnp.sum((jnp.cumsum(onehot, axis=0) - onehot) * onehot, axis=1)
    counts = jnp.sum(onehot, axis=0)
    padded = (counts + tm - 1) // tm * tm
    ends = jnp.cumsum(padded)
    starts = ends - padded
    pos = starts[flat_e] + rank
    token = jnp.arange(T * top_k, dtype=jnp.int32) // top_k
    row_src = jnp.zeros((n_tiles * tm,), jnp.int32).at[pos].set(token)
    row_w = jnp.zeros((n_tiles * tm,), F32).at[pos].set(top_w.reshape(-1))
    tile_start = jnp.arange(n_tiles, dtype=jnp.int32) * tm
    tile_expert = jnp.minimum(jnp.sum((tile_start[:, None] >= ends[None, :]).astype(jnp.int32), axis=1),
                              N_EXPERTS - 1)
    n_used = (ends[-1] // tm).reshape(1)
    return pos.astype(jnp.int32), row_src, row_w.reshape(-1, 1), tile_expert.astype(jnp.int32), n_used.astype(jnp.int32)


def _q_up_weight(w):
    r = w.shape[0]
    w = w.reshape(r, MLA_HEADS, MLA_QK)
    nope, rp = w[:, :, :MLA_NOPE], w[:, :, MLA_NOPE:]
    half = MLA_ROPE // 2
    rot = jnp.concatenate([-rp[:, :, half:], rp[:, :, :half]], axis=2)
    z = jnp.zeros((r, MLA_HEADS, V7X_LANES - MLA_ROPE), w.dtype)
    return jnp.concatenate([nope, rp, z, rot, z], axis=2).reshape(r, -1).astype(BF16)


def _kv_up_weight(w):
    r = w.shape[0]
    w = w.reshape(r, MLA_HEADS, MLA_NOPE + MLA_V)
    return jnp.concatenate([w[:, :, :MLA_NOPE].reshape(r, -1), w[:, :, MLA_NOPE:].reshape(r, -1)],
                           axis=1).astype(BF16)


def _rope_tables(positions):
    pos = positions.reshape(-1).astype(F32)[:, None]
    inv_m = ROPE_BASE ** (-jnp.arange(0, MLA_ROPE, 2, dtype=F32) / MLA_ROPE)
    am = pos * inv_m
    zm = jnp.zeros((pos.shape[0], V7X_LANES - MLA_ROPE), F32)
    cos_m = jnp.concatenate([jnp.cos(am), jnp.cos(am), zm], axis=1)
    sin_m = jnp.concatenate([jnp.sin(am), jnp.sin(am), zm], axis=1)
    inv_r = ROPE_BASE ** (-jnp.arange(0, RET_DK, 2, dtype=F32) / RET_DK)
    ar = pos * inv_r
    cos_r = jnp.concatenate([jnp.cos(ar), jnp.cos(ar)], axis=1)
    sin_r = jnp.concatenate([-jnp.sin(ar), jnp.sin(ar)], axis=1)
    return cos_m, sin_m, cos_r, sin_r


def kernel(x, c, positions, w_ada, b_ada, ada_table, norm_g, w_in, mla_q_norm, mla_w_q_up, mla_kv_norm,
           mla_w_kv_up, hg_lb_logits, hg_norm, ret_norm, w_branch, w_out, ffn_w1, ffn_w3, ffn_w2,
           moe_router, moe_w1, moe_w3, moe_w2, final_norm):
    B, S, D = x.shape
    T = B * S
    depth = norm_g.shape[0]
    top_k = 2
    assert depth % 2 == 0, "the final norm is fused into the expert combine of the last (odd) layer"
    assert S % 512 == 0 and D % 512 == 0

    mod = _ada(c, w_ada, b_ada, ada_table)
    cos_m, sin_m, cos_r, sin_r = _rope_tables(positions)

    n_cqkv = MLA_Q_RANK + MLA_KV_RANK
    n_lat = n_cqkv + MLA_ROPE
    n_hg = 4 * HG_WIDTH
    n_ret = 2 * RET_HEADS * RET_DK + 2 * RET_HEADS * RET_DV

    d_in = w_in.shape[2]
    w_in_t = jnp.swapaxes(w_in, 1, 2).reshape(depth, d_in // _WT_GROUP, _WT_GROUP, D)

    xf = x.reshape(T, D)
    pending = None
    out = None
    for l in range(depth):
        shift1, scale1, gate1, shift2, scale2, gate2 = (mod[l, :, j] for j in range(N_MOD))
        if pending is None:
            (h,) = _norm(xf, norm_g[l, 0], scale1, shift1, S)
        else:
            xf, h = _norm(xf, norm_g[l, 0], scale1, shift1, S, resid=pending)
            pending = None

        lat = _matmul(h, w_in_t, l, 0, n_cqkv, transposed=True)
        kr_raw = _matmul(h, w_in_t, l, n_cqkv, V7X_LANES, transposed=True, tn=V7X_LANES)
        hg = _matmul(h, w_in_t, l, n_lat, n_hg, transposed=True)
        rt = _matmul(h, w_in_t, l, n_lat + n_hg, n_ret, transposed=True)
        gates = _matmul(h, w_in_t, l, n_lat + n_hg + n_ret, N_BRANCH * D, transposed=True, mode="sigmoid",
                        out_dtype=BF16)

        y_mla = _mla(lat, kr_raw, cos_m, sin_m, mla_q_norm[l], _q_up_weight(mla_w_q_up[l]),
                     mla_kv_norm[l], _kv_up_weight(mla_w_kv_up[l]), B, S)
        y_hg = _hgrn2(hg, hg_lb_logits, hg_norm[l], l, B, S)
        y_ret = _retention(rt, cos_r, sin_r, ret_norm[l], B, S)

        merged = _merge(y_mla, y_hg, y_ret, w_branch, l, gates)
        xf = _matmul(merged, w_out, l, 0, D, mode="resid", resid=xf, gate=gate1, seq=S)

        if l % 2 == 0:
            (h2,) = _norm(xf, norm_g[l, 1], scale2, shift2, S)
            e = l // 2
            y = _ffn(h2, ffn_w1.astype(BF16), ffn_w3.astype(BF16), ffn_w2.astype(BF16), e, tm=512, tf=256)
            pending = (y, gate2)
        else:
            e = l // 2
            h2, ti, tw = _norm(xf, norm_g[l, 1], scale2, shift2, S, router=moe_router[e])
            tm = 512
            n_tiles = T * top_k // tm + N_EXPERTS
            pos, row_src, row_w, tile_expert, n_used = _route(ti[:, :top_k], tw[:, :top_k], tm, n_tiles)
            xs = _gather_rows(h2, row_src, n_tiles * tm)
            ys = _moe_experts(xs, moe_w1, moe_w3, moe_w2, e, tile_expert, n_used, row_w, tm=tm, tf=512, tn=1024)
            assert l == depth - 1
            out = _combine(pos, xf, gate2, final_norm, ys, S, top_k)
    return out.reshape(B, S, D)
```

```python
import functools
import math

import jax
import jax.numpy as jnp
from jax import lax
from jax.experimental import pallas as pl
from jax.experimental.pallas import tpu as pltpu

F32 = jnp.float32
BF16 = jnp.bfloat16

V7X_LANES = 128
V7X_SUBLANES = 8
V7X_VMEM_BYTES = 64 * 1024 * 1024
V7X_VMEM_LIMIT = V7X_VMEM_BYTES - 8 * 1024 * 1024

EPS = 1e-6
ROPE_BASE = 10000.0
MASK_VALUE = -1e30

MLA_HEADS = 16
MLA_NOPE = 128
MLA_ROPE = 64
MLA_V = 128
MLA_Q_RANK = 1024
MLA_KV_RANK = 512
MLA_QK = MLA_NOPE + MLA_ROPE
MLA_QPAD = 2 * V7X_LANES

HG_HEADS = 16
HG_D = 128
HG_WIDTH = HG_HEADS * HG_D
HG_CHUNK = 64
HG_SUB = 8
HG_MIN_F = 1e-6

RET_HEADS = 8
RET_DK = 128
RET_DV = 256
RET_CHUNK = 128

N_BRANCH = 3
N_EXPERTS = 8
N_MOD = 6


def _params(semantics):
    return pltpu.CompilerParams(dimension_semantics=semantics, vmem_limit_bytes=V7X_VMEM_LIMIT)


def _rms(x, gain):
    return x * lax.rsqrt(jnp.mean(x * x, axis=-1, keepdims=True) + EPS) * gain


def _silu(x):
    return x * jax.nn.sigmoid(x)


def _dot_nt(a, b):
    return lax.dot_general(a, b, (((1,), (1,)), ((), ())), preferred_element_type=F32)


def _dot_tn(a, b):
    return jnp.dot(a.T, b, preferred_element_type=F32)


def _ada_kernel(c_ref, w_ref, b_ref, tab_ref, o_ref, acc_ref):
    k = pl.program_id(0)

    @pl.when(k == 0)
    def _():
        acc_ref[...] = jnp.zeros_like(acc_ref)

    s = _silu(c_ref[...]).astype(BF16)
    acc_ref[...] += jnp.dot(s, w_ref[...].astype(BF16), preferred_element_type=F32)

    @pl.when(k == pl.num_programs(0) - 1)
    def _():
        acc = acc_ref[...] + b_ref[...]
        for l in range(o_ref.shape[0]):
            o_ref[l] = acc + tab_ref[l]


def _ada(c, w_ada, b_ada, ada_table):
    B, D = c.shape
    depth = ada_table.shape[0]
    N = w_ada.shape[1]
    tk = V7X_LANES
    cp = jnp.zeros((V7X_SUBLANES, D), F32).at[:B].set(c)
    out = pl.pallas_call(
        _ada_kernel,
        out_shape=jax.ShapeDtypeStruct((depth, V7X_SUBLANES, N), F32),
        grid=(D // tk,),
        in_specs=[
            pl.BlockSpec((V7X_SUBLANES, tk), lambda k: (0, k)),
            pl.BlockSpec((tk, N), lambda k: (k, 0)),
            pl.BlockSpec((1, N), lambda k: (0, 0)),
            pl.BlockSpec((depth, 1, N), lambda k: (0, 0, 0)),
        ],
        out_specs=pl.BlockSpec((depth, V7X_SUBLANES, N), lambda k: (0, 0, 0)),
        scratch_shapes=[pltpu.VMEM((V7X_SUBLANES, N), F32)],
        compiler_params=_params(("arbitrary",)),
        name="ada_mod",
    )(cp, w_ada, b_ada.reshape(1, N), ada_table.reshape(depth, 1, N))
    return out[:, :B].reshape(depth, B, N_MOD, D)


def _norm_kernel(*refs, has_resid, has_router, n_experts):
    it = iter(refs)
    x_ref = next(it)
    if has_resid:
        y_ref, gt_ref = next(it), next(it)
    g_ref, sc_ref, sh_ref = next(it), next(it), next(it)
    if has_router:
        r_ref = next(it)
    if has_resid:
        xo_ref = next(it)
    h_ref = next(it)
    if has_router:
        ti_ref, tw_ref = next(it), next(it)

    x = x_ref[...]
    if has_resid:
        x = x + gt_ref[0] * y_ref[...]
        xo_ref[...] = x
    h = _rms(x, g_ref[...]) * (1.0 + sc_ref[0]) + sh_ref[0]
    if not has_router:
        h_ref[...] = h.astype(h_ref.dtype)
    else:
        h_ref[...] = _pack_bf16_pairs(h)
        logits = jnp.dot(h, r_ref[...], preferred_element_type=F32, precision=lax.Precision.HIGHEST)
        lane = lax.broadcasted_iota(jnp.int32, logits.shape, 1).astype(F32)
        neg = jnp.float32(-jnp.inf)
        big = jnp.float32(V7X_LANES)
        lg = jnp.where(lane < n_experts, logits, neg)
        m1 = jnp.max(lg, axis=-1, keepdims=True)
        i1 = jnp.min(jnp.where(lg == m1, lane, big), axis=-1, keepdims=True)
        lg2 = jnp.where(lane == i1, neg, lg)
        m2 = jnp.max(lg2, axis=-1, keepdims=True)
        i2 = jnp.min(jnp.where(lg2 == m2, lane, big), axis=-1, keepdims=True)
        e = jnp.exp(m2 - m1)
        w1 = 1.0 / (1.0 + e)
        w2 = e / (1.0 + e)
        ti_ref[...] = jnp.where(lane == 0.0, i1, jnp.where(lane == 1.0, i2, 0.0)).astype(jnp.int32)
        tw_ref[...] = jnp.where(lane == 0.0, w1, jnp.where(lane == 1.0, w2, 0.0))


def _norm(x, gain, scale, shift, seq, *, resid=None, router=None):
    T, D = x.shape
    B = scale.shape[0]
    tm = 256
    row = pl.BlockSpec((tm, D), lambda i: (i, 0))
    per_batch = pl.BlockSpec((1, 1, D), lambda i: (i * tm // seq, 0, 0))
    args, in_specs = [x], [row]
    if resid is not None:
        args += [resid[0], resid[1].reshape(B, 1, D)]
        in_specs += [row, per_batch]
    args += [gain.reshape(1, D), scale.reshape(B, 1, D), shift.reshape(B, 1, D)]
    in_specs += [pl.BlockSpec((1, D), lambda i: (0, 0)), per_batch, per_batch]
    out_shape, out_specs = [], []
    if router is not None:
        n_experts = router.shape[1]
        rp = jnp.zeros((D, V7X_LANES), F32).at[:, :n_experts].set(router)
        args.append(rp)
        in_specs.append(pl.BlockSpec((D, V7X_LANES), lambda i: (0, 0)))
    else:
        n_experts = 0
    if resid is not None:
        out_shape.append(jax.ShapeDtypeStruct((T, D), F32))
        out_specs.append(row)
    if router is None:
        out_shape.append(jax.ShapeDtypeStruct((T, D), BF16))
        out_specs.append(row)
    else:
        out_shape.append(jax.ShapeDtypeStruct((T, D // 2), jnp.uint32))
        out_specs.append(pl.BlockSpec((tm, D // 2), lambda i: (i, 0)))
    if router is not None:
        small = pl.BlockSpec((tm, V7X_LANES), lambda i: (i, 0))
        out_shape += [jax.ShapeDtypeStruct((T, V7X_LANES), jnp.int32), jax.ShapeDtypeStruct((T, V7X_LANES), F32)]
        out_specs += [small, small]
    return pl.pallas_call(
        functools.partial(_norm_kernel, has_resid=resid is not None, has_router=router is not None,
                          n_experts=n_experts),
        out_shape=out_shape,
        grid=(T // tm,),
        in_specs=in_specs,
        out_specs=out_specs,
        compiler_params=_params(("arbitrary",)),
        name="ada_norm",
    )(*args)


_W_CAST_ROWS = 512


_WT_GROUP = 64


def _mm_kernel(*refs, mode, n_groups):
    it = iter(refs)
    a_ref = next(it)
    w_refs = [next(it) for _ in range(max(n_groups, 1))]
    if mode == "resid":
        r_ref, g_ref = next(it), next(it)
    o_ref, wscr = next(it), next(it)
    c = pl.program_id(2)

    @pl.when(pl.program_id(1) == 0)
    def _():
        if n_groups:
            for g, w_ref in enumerate(w_refs):
                wscr[c, g * _WT_GROUP:(g + 1) * _WT_GROUP, :] = w_ref[0, 0].astype(BF16)
        else:
            for r0 in range(0, wscr.shape[1], _W_CAST_ROWS):
                wscr[c, r0:r0 + _W_CAST_ROWS, :] = w_refs[0][0, r0:r0 + _W_CAST_ROWS, :].astype(BF16)

    if n_groups:
        acc = _dot_nt(a_ref[...], wscr[c])
    else:
        acc = jnp.dot(a_ref[...], wscr[c], preferred_element_type=F32)
    if mode == "plain":
        o_ref[...] = acc.astype(o_ref.dtype)
    elif mode == "sigmoid":
        o_ref[...] = jax.nn.sigmoid(acc).astype(o_ref.dtype)
    else:
        o_ref[...] = r_ref[...] + g_ref[0] * acc


def _matmul(a, w, layer, col0, n, *, transposed=False, mode="plain", out_dtype=F32, resid=None, gate=None,
            seq=None, tm=1024, tn=512):
    M, K = a.shape
    assert n % tn == 0 and M % tm == 0
    pair = 2 if (n // tn) % 2 == 0 else 1

    def col_tile(jj, i, c):
        return jj * pair + jnp.where(i == 0, c, pair - 1)

    args = [a]
    in_specs = [pl.BlockSpec((tm, K), lambda jj, i, c: (i, 0))]
    if transposed:
        assert col0 % _WT_GROUP == 0 and tn % _WT_GROUP == 0
        n_groups = tn // _WT_GROUP
        g0 = col0 // _WT_GROUP
        for g in range(n_groups):
            args.append(w)
            in_specs.append(pl.BlockSpec(
                (1, 1, _WT_GROUP, K),
                lambda jj, i, c, g=g: (layer, g0 + col_tile(jj, i, c) * n_groups + g, 0, 0)))
        wscr = pltpu.VMEM((pair, tn, K), BF16)
    else:
        assert col0 % tn == 0 and K % _W_CAST_ROWS == 0
        n_groups = 0
        args.append(w)
        in_specs.append(pl.BlockSpec((1, K, tn), lambda jj, i, c: (layer, 0, col0 // tn + col_tile(jj, i, c))))
        wscr = pltpu.VMEM((pair, K, tn), BF16)
    out_spec = pl.BlockSpec((tm, tn), lambda jj, i, c: (i, jj * pair + c))
    if mode == "resid":
        B = gate.shape[0]
        args += [resid, gate.reshape(B, 1, n)]
        in_specs += [out_spec, pl.BlockSpec((1, 1, tn), lambda jj, i, c: (i * tm // seq, 0, jj * pair + c))]
    return pl.pallas_call(
        functools.partial(_mm_kernel, mode=mode, n_groups=n_groups),
        out_shape=jax.ShapeDtypeStruct((M, n), out_dtype),
        grid=(n // (tn * pair), M // tm, pair),
        in_specs=in_specs,
        out_specs=out_spec,
        scratch_shapes=[wscr],
        compiler_params=_params(("arbitrary", "arbitrary", "arbitrary")),
        name="matmul_" + mode,
    )(*args)


def _mla_q_kernel(cq_ref, g_ref, w_ref, cos_ref, sin_ref, o_ref, a_scr, *, heads_per_step, scale):
    @pl.when(pl.program_id(1) == 0)
    def _():
        a_scr[...] = _rms(cq_ref[...], g_ref[...]).astype(BF16)

    acc = jnp.dot(a_scr[...], w_ref[...], preferred_element_type=F32)
    cos, sin = cos_ref[...], sin_ref[...]
    L = V7X_LANES
    for h in range(heads_per_step):
        nope = acc[:, 3 * L * h:3 * L * h + L]
        rp = acc[:, 3 * L * h + L:3 * L * h + 2 * L]
        rt = acc[:, 3 * L * h + 2 * L:3 * L * h + 3 * L]
        o_ref[:, 2 * L * h:2 * L * h + L] = (nope * scale).astype(BF16)
        o_ref[:, 2 * L * h + L:2 * L * h + 2 * L] = ((rp * cos + rt * sin) * scale).astype(BF16)


def _mla_kv_kernel(ckv_ref, g_ref, w_ref, kraw_ref, cos_ref, sin_ref, kv_ref, kr_ref, a_scr):
    @pl.when(pl.program_id(1) == 0)
    def _():
        a_scr[...] = _rms(ckv_ref[...], g_ref[...]).astype(BF16)
        x = kraw_ref[...]
        half = MLA_ROPE // 2
        lane = lax.broadcasted_iota(jnp.int32, x.shape, 1)
        swapped = jnp.where(lane < half, -pltpu.roll(x, V7X_LANES - half, 1), pltpu.roll(x, half, 1))
        roped = x * cos_ref[...] + swapped * sin_ref[...]
        kr_ref[...] = jnp.where(lane < MLA_ROPE, roped, 0.0).astype(BF16)

    kv_ref[...] = jnp.dot(a_scr[...], w_ref[...], preferred_element_type=F32).astype(BF16)


def _mla_attn_kernel(q_ref, k_ref, kr_ref, v_ref, o_ref, *, tq, n_heads):
    i = pl.program_id(2)
    qs = [q_ref[:, h * MLA_QPAD:(h + 1) * MLA_QPAD] for h in range(n_heads)]

    def block(j, carry, masked):
        off = pl.multiple_of(j * tq, tq)
        kr = kr_ref[pl.ds(off, tq), :]
        out = []
        for h in range(n_heads):
            m, l, acc = carry[h]
            kk = jnp.concatenate([k_ref[pl.ds(off, tq), h * MLA_NOPE:(h + 1) * MLA_NOPE], kr], axis=1)
            s = _dot_nt(qs[h], kk)
            if masked:
                r = lax.broadcasted_iota(jnp.int32, s.shape, 0)
                c = lax.broadcasted_iota(jnp.int32, s.shape, 1)
                s = jnp.where(c <= r, s, MASK_VALUE)
            m_new = jnp.maximum(m, jnp.max(s, axis=-1, keepdims=True))
            alpha = jnp.exp(m - m_new)
            p = jnp.exp(s - m_new)
            l = alpha * l + jnp.sum(p, axis=-1, keepdims=True)
            v = v_ref[pl.ds(off, tq), h * MLA_V:(h + 1) * MLA_V]
            acc = alpha * acc + jnp.dot(p.astype(BF16), v, preferred_element_type=F32)
            out.append((m_new, l, acc))
        return tuple(out)

    carry = tuple((jnp.full((tq, 1), MASK_VALUE, F32), jnp.zeros((tq, 1), F32), jnp.zeros((tq, MLA_V), F32))
                  for _ in range(n_heads))
    carry = lax.fori_loop(0, i, lambda j, c: block(j, c, False), carry)
    carry = block(i, carry, True)
    for h in range(n_heads):
        _, l, acc = carry[h]
        o_ref[:, h * MLA_V:(h + 1) * MLA_V] = (acc / l).astype(o_ref.dtype)


def _mla(lat, kr_raw, cos_m, sin_m, q_norm, wq, kv_norm, wkv, B, S):
    T = lat.shape[0]
    L = V7X_LANES
    tm = 512
    hps = 4
    scale = MLA_QK ** -0.5
    q = pl.pallas_call(
        functools.partial(_mla_q_kernel, heads_per_step=hps, scale=scale),
        out_shape=jax.ShapeDtypeStruct((T, MLA_HEADS * MLA_QPAD), BF16),
        grid=(T // tm, MLA_HEADS // hps),
        in_specs=[
            pl.BlockSpec((tm, MLA_Q_RANK), lambda i, j: (i, 0)),
            pl.BlockSpec((1, MLA_Q_RANK), lambda i, j: (0, 0)),
            pl.BlockSpec((MLA_Q_RANK, hps * 3 * L), lambda i, j: (0, j)),
            pl.BlockSpec((tm, L), lambda i, j: (i, 0)),
            pl.BlockSpec((tm, L), lambda i, j: (i, 0)),
        ],
        out_specs=pl.BlockSpec((tm, hps * MLA_QPAD), lambda i, j: (i, j)),
        scratch_shapes=[pltpu.VMEM((tm, MLA_Q_RANK), BF16)],
        compiler_params=_params(("arbitrary", "arbitrary")),
        name="mla_q",
    )(lat, q_norm.reshape(1, -1), wq, cos_m, sin_m)

    tn = 1024
    kv_w = 2 * MLA_HEADS * MLA_NOPE
    ckv_blk = MLA_Q_RANK // MLA_KV_RANK
    kv, kr = pl.pallas_call(
        _mla_kv_kernel,
        out_shape=[jax.ShapeDtypeStruct((T, kv_w), BF16), jax.ShapeDtypeStruct((T, L), BF16)],
        grid=(T // tm, kv_w // tn),
        in_specs=[
            pl.BlockSpec((tm, MLA_KV_RANK), lambda i, j: (i, ckv_blk)),
            pl.BlockSpec((1, MLA_KV_RANK), lambda i, j: (0, 0)),
            pl.BlockSpec((MLA_KV_RANK, tn), lambda i, j: (0, j)),
            pl.BlockSpec((tm, L), lambda i, j: (i, 0)),
            pl.BlockSpec((tm, L), lambda i, j: (i, 0)),
            pl.BlockSpec((tm, L), lambda i, j: (i, 0)),
        ],
        out_specs=[pl.BlockSpec((tm, tn), lambda i, j: (i, j)), pl.BlockSpec((tm, L), lambda i, j: (i, 0))],
        scratch_shapes=[pltpu.VMEM((tm, MLA_KV_RANK), BF16)],
        compiler_params=_params(("arbitrary", "arbitrary")),
        name="mla_kv",
    )(lat, kv_norm.reshape(1, -1), wkv, kr_raw, cos_m, sin_m)

    tq = 512
    nq = S // tq
    nh = 4
    hg = MLA_HEADS // nh
    return pl.pallas_call(
        functools.partial(_mla_attn_kernel, tq=tq, n_heads=nh),
        out_shape=jax.ShapeDtypeStruct((T, MLA_HEADS * MLA_V), BF16),
        grid=(B, hg, nq),
        in_specs=[
            pl.BlockSpec((tq, nh * MLA_QPAD), lambda b, h, i: (b * nq + i, h)),
            pl.BlockSpec((S, nh * MLA_NOPE), lambda b, h, i: (b, h)),
            pl.BlockSpec((S, L), lambda b, h, i: (b, 0)),
            pl.BlockSpec((S, nh * MLA_V), lambda b, h, i: (b, hg + h)),
        ],
        out_specs=pl.BlockSpec((tq, nh * MLA_V), lambda b, h, i: (b * nq + i, h)),
        compiler_params=_params(("arbitrary", "arbitrary", "arbitrary")),
        name="mla_attn",
    )(q, kv, kr, kv)


def _hgrn2_kernel(q_ref, f_ref, i_ref, gate_ref, lbl_ref, ng_ref, o_ref, st_scr, q_scr, k_scr, b_scr, *, layer):
    tb = q_ref.shape[0]
    C = HG_CHUNK

    @pl.when(pl.program_id(2) == 0)
    def _():
        st_scr[...] = jnp.zeros_like(st_scr)

    lg = lbl_ref[...]
    e = jnp.exp(lg - jnp.max(lg, axis=0, keepdims=True))
    p = e / jnp.sum(e, axis=0, keepdims=True)
    lb = jnp.zeros((1, HG_D), F32)
    for l in range(1, layer + 1):
        lb = lb + p[l:l + 1, :]

    fr = f_ref[...]
    f = lb + (1.0 - lb) * jax.nn.sigmoid(fr)
    logf = jnp.log(jnp.maximum(f, HG_MIN_F))
    q_scr[...] = _silu(q_ref[...])
    k_scr[...] = (1.0 - lb) * jax.nn.sigmoid(-fr)

    hi = logf.astype(BF16)
    r1 = logf - hi.astype(F32)
    mid = r1.astype(BF16)
    lo = (r1 - mid.astype(F32)).astype(BF16)
    rr = lax.broadcasted_iota(jnp.int32, (C, C), 0)
    cc = lax.broadcasted_iota(jnp.int32, (C, C), 1)
    causal = cc <= rr
    tril = jnp.where(causal, 1.0, 0.0).astype(BF16)
    for c in range(tb // C):
        sl = slice(c * C, (c + 1) * C)
        parts = jnp.concatenate([hi[sl], mid[sl], lo[sl]], axis=1)
        cs = jnp.dot(tril, parts, preferred_element_type=F32)
        b_scr[sl, :] = cs[:, :HG_D] + cs[:, HG_D:2 * HG_D] + cs[:, 2 * HG_D:]

    st = st_scr[...]
    ng = ng_ref[...]
    for c in range(tb // C):
        r0 = c * C
        b_c = b_scr[r0:r0 + C, :]
        q_c = q_scr[r0:r0 + C, :]
        k_c = k_scr[r0:r0 + C, :]
        v_c = i_ref[r0:r0 + C, :]
        b_last = b_scr[r0 + C - 1:r0 + C, :]
        rows = []
        for blk in range(C // HG_SUB):
            s0 = r0 + blk * HG_SUB
            n = (blk + 1) * HG_SUB
            ref = b_scr[s0 + HG_SUB // 2 - 1:s0 + HG_SUB // 2, :]
            q_b = q_scr[s0:s0 + HG_SUB, :] * jnp.exp(b_scr[s0:s0 + HG_SUB, :] - ref)
            k_b = k_scr[r0:r0 + n, :] * jnp.exp(ref - b_scr[r0:r0 + n, :])
            if n < C:
                k_b = jnp.concatenate([k_b, jnp.zeros((C - n, HG_D), F32)], axis=0)
            rows.append(_dot_nt(q_b, k_b))
        scores = jnp.where(causal, jnp.concatenate(rows, axis=0), 0.0)
        o = jnp.dot(scores, v_c, preferred_element_type=F32) + _dot_nt(q_c * jnp.exp(b_c), st)
        st = st * jnp.exp(b_last) + _dot_tn(v_c, k_c * jnp.exp(b_last - b_c))
        y = _rms(o, ng) * _silu(gate_ref[r0:r0 + C, :])
        o_ref[r0:r0 + C, :] = y.astype(o_ref.dtype)
    st_scr[...] = st


def _hgrn2(hg, lb_logits, norm_g, layer, B, S):
    T = hg.shape[0]
    tb = 512
    nb = S // tb
    depth = lb_logits.shape[0]

    def part(p):
        return pl.BlockSpec((tb, HG_D), lambda b, h, j: (b * nb + j, p * HG_HEADS + h))

    return pl.pallas_call(
        functools.partial(_hgrn2_kernel, layer=layer),
        out_shape=jax.ShapeDtypeStruct((T, HG_WIDTH), BF16),
        grid=(B, HG_HEADS, nb),
        in_specs=[part(0), part(1), part(2), part(3),
                  pl.BlockSpec((depth, HG_D), lambda b, h, j: (0, h)),
                  pl.BlockSpec((1, HG_D), lambda b, h, j: (0, h))],
        out_specs=pl.BlockSpec((tb, HG_D), lambda b, h, j: (b * nb + j, h)),
        scratch_shapes=[pltpu.VMEM((HG_D, HG_D), F32), pltpu.VMEM((tb, HG_D), F32),
                        pltpu.VMEM((tb, HG_D), F32), pltpu.VMEM((tb, HG_D), F32)],
        compiler_params=_params(("arbitrary", "arbitrary", "arbitrary")),
        name="hgrn2",
    )(hg, hg, hg, hg, lb_logits, norm_g.reshape(1, -1))


def _ret_kernel(q_ref, k_ref, v_ref, gate_ref, cos_ref, sin_ref, lg_ref, ng_ref, o_ref, st_scr):
    tb = q_ref.shape[0]
    C = RET_CHUNK

    @pl.when(pl.program_id(2) == 0)
    def _():
        st_scr[...] = jnp.zeros_like(st_scr)

    cos, sin = cos_ref[...], sin_ref[...]
    half = RET_DK // 2
    xq, xk = q_ref[...], k_ref[...]
    q = xq * cos + pltpu.roll(xq, half, 1) * sin
    k = (xk * cos + pltpu.roll(xk, half, 1) * sin) * (RET_DK ** -0.5)

    lg = lg_ref[0]
    rr = lax.broadcasted_iota(jnp.int32, (C, C), 0)
    cc = lax.broadcasted_iota(jnp.int32, (C, C), 1)
    rel = (rr - cc).astype(F32)
    decay = jnp.where(rel >= 0.0, jnp.exp(lg * jnp.maximum(rel, 0.0)), 0.0)
    idx = rr.astype(F32)
    q_dec = jnp.exp(lg * (idx + 1.0))
    k_dec = jnp.exp(lg * (C - 1.0 - idx))
    chunk_dec = jnp.exp(lg * float(C))

    st = st_scr[...]
    ng = ng_ref[...]
    for c in range(tb // C):
        sl = slice(c * C, (c + 1) * C)
        q_c, k_c = q[sl], k[sl]
        v_c = v_ref[sl, :]
        s = _dot_nt(q_c.astype(BF16), k_c.astype(BF16)) * decay
        o = jnp.dot(s.astype(BF16), v_c.astype(BF16), preferred_element_type=F32)
        o = o + _dot_nt((q_c * q_dec).astype(BF16), st.astype(BF16))
        st = st * chunk_dec + jnp.dot(v_c.T.astype(BF16), (k_c * k_dec).astype(BF16),
                                      preferred_element_type=F32)
        y = _rms(o, ng) * _silu(gate_ref[sl, :])
        o_ref[sl, :] = y.astype(o_ref.dtype)
    st_scr[...] = st


def _retention(rt, cos_r, sin_r, norm_g, B, S):
    T = rt.shape[0]
    tb = 512
    nb = S // tb
    H = RET_HEADS
    v_blk0 = 2 * H * RET_DK // RET_DV
    log_gamma = jnp.log1p(-(2.0 ** (-5.0 - jnp.arange(H, dtype=F32))))
    lg = jnp.broadcast_to(log_gamma[:, None, None], (H, 1, RET_CHUNK))
    return pl.pallas_call(
        _ret_kernel,
        out_shape=jax.ShapeDtypeStruct((T, H * RET_DV), BF16),
        grid=(B, H, nb),
        in_specs=[
            pl.BlockSpec((tb, RET_DK), lambda b, h, j: (b * nb + j, h)),
            pl.BlockSpec((tb, RET_DK), lambda b, h, j: (b * nb + j, H + h)),
            pl.BlockSpec((tb, RET_DV), lambda b, h, j: (b * nb + j, v_blk0 + h)),
            pl.BlockSpec((tb, RET_DV), lambda b, h, j: (b * nb + j, v_blk0 + H + h)),
            pl.BlockSpec((tb, RET_DK), lambda b, h, j: (b * nb + j, 0)),
            pl.BlockSpec((tb, RET_DK), lambda b, h, j: (b * nb + j, 0)),
            pl.BlockSpec((1, 1, RET_CHUNK), lambda b, h, j: (h, 0, 0)),
            pl.BlockSpec((1, RET_DV), lambda b, h, j: (0, h)),
        ],
        out_specs=pl.BlockSpec((tb, RET_DV), lambda b, h, j: (b * nb + j, h)),
        scratch_shapes=[pltpu.VMEM((RET_DV, RET_DK), F32)],
        compiler_params=_params(("arbitrary", "arbitrary", "arbitrary")),
        name="retention",
    )(rt, rt, rt, rt, cos_r, sin_r, lg, norm_g.reshape(1, -1))


def _merge_kernel(ya_ref, yb_ref, yc_ref, w_ref, ga_ref, gb_ref, gc_ref, o_ref, wscr):
    @pl.when(pl.program_id(1) == 0)
    def _():
        for br in range(N_BRANCH):
            for r0 in range(0, wscr.shape[1], _W_CAST_ROWS):
                wscr[br, r0:r0 + _W_CAST_ROWS, :] = w_ref[0, br, r0:r0 + _W_CAST_ROWS, :].astype(BF16)

    acc = ga_ref[...].astype(F32) * jnp.dot(ya_ref[...], wscr[0], preferred_element_type=F32)
    acc = acc + gb_ref[...].astype(F32) * jnp.dot(yb_ref[...], wscr[1], preferred_element_type=F32)
    acc = acc + gc_ref[...].astype(F32) * jnp.dot(yc_ref[...], wscr[2], preferred_element_type=F32)
    o_ref[...] = acc.astype(o_ref.dtype)


def _merge(ya, yb, yc, w_branch, layer, gates):
    T, K = ya.shape
    D = w_branch.shape[3]
    tm, tn = 512, 512
    nj = D // tn
    ybs = pl.BlockSpec((tm, K), lambda j, i: (i, 0))
    return pl.pallas_call(
        _merge_kernel,
        out_shape=jax.ShapeDtypeStruct((T, D), BF16),
        grid=(nj, T // tm),
        in_specs=[ybs, ybs, ybs,
                  pl.BlockSpec((1, N_BRANCH, K, tn), lambda j, i: (layer, 0, 0, j)),
                  pl.BlockSpec((tm, tn), lambda j, i: (i, j)),
                  pl.BlockSpec((tm, tn), lambda j, i: (i, nj + j)),
                  pl.BlockSpec((tm, tn), lambda j, i: (i, 2 * nj + j))],
        out_specs=pl.BlockSpec((tm, tn), lambda j, i: (i, j)),
        scratch_shapes=[pltpu.VMEM((N_BRANCH, K, tn), BF16)],
        compiler_params=_params(("arbitrary", "arbitrary")),
        name="merge",
    )(ya, yb, yc, w_branch, gates, gates, gates)


def _ffn_kernel(x_ref, w1_ref, w3_ref, w2_ref, o_ref):
    @pl.when(pl.program_id(1) == 0)
    def _():
        o_ref[...] = jnp.zeros_like(o_ref)

    x = x_ref[...]
    g = jnp.dot(x, w1_ref[0], preferred_element_type=F32)
    u = jnp.dot(x, w3_ref[0], preferred_element_type=F32)
    o_ref[...] += jnp.dot((_silu(g) * u).astype(BF16), w2_ref[0], preferred_element_type=F32)


def _ffn(x, w1, w3, w2, e, *, tm, tf):
    M, D = x.shape
    F = w1.shape[2]
    assert M % tm == 0 and F % tf == 0
    w13 = pl.BlockSpec((1, D, tf), lambda i, f: (e, 0, f))
    return pl.pallas_call(
        _ffn_kernel,
        out_shape=jax.ShapeDtypeStruct((M, D), F32),
        grid=(M // tm, F // tf),
        in_specs=[pl.BlockSpec((tm, D), lambda i, f: (i, 0)), w13, w13,
                  pl.BlockSpec((1, tf, D), lambda i, f: (e, f, 0))],
        out_specs=pl.BlockSpec((tm, D), lambda i, f: (i, 0)),
        compiler_params=_params(("arbitrary", "arbitrary")),
        name="ffn_dense",
    )(x, w1, w3, w2)


def _expert_changed(te_ref, i):
    return jnp.logical_or(i == 0, te_ref[i] != te_ref[jnp.maximum(i - 1, 0)])


def _expert_weights(te_ref, nx_ref, w_hbms, wbuf, sem, slot_ref, dsts, *, layer, col_tile):
    sweep = pl.program_id(0)
    i = pl.program_id(1)

    def copies(expert, swp, slot):
        col = pl.multiple_of(swp * col_tile, col_tile)
        return [pltpu.make_async_copy(w.at[layer, expert, :, pl.ds(col, col_tile)], wbuf.at[slot, k], sem.at[slot])
                for k, w in enumerate(w_hbms)]

    @pl.when(_expert_changed(te_ref, i))
    def _():
        @pl.when(jnp.logical_and(sweep == 0, i == 0))
        def _():
            slot_ref[0] = 0
            for cp in copies(te_ref[0], 0, 0):
                cp.start()

        slot = slot_ref[0]
        for cp in copies(te_ref[i], sweep, slot):
            cp.wait()
        nxt = nx_ref[i]

        @pl.when(nxt >= 0)
        def _():
            for cp in copies(nxt, sweep, 1 - slot):
                cp.start()

        @pl.when(jnp.logical_and(nxt < 0, sweep + 1 < pl.num_programs(0)))
        def _():
            for cp in copies(te_ref[0], sweep + 1, 1 - slot):
                cp.start()

        for k, dst in enumerate(dsts):
            for r0 in range(0, dst.shape[0], _W_CAST_ROWS):
                dst[r0:r0 + _W_CAST_ROWS, :] = wbuf[slot, k, r0:r0 + _W_CAST_ROWS, :].astype(BF16)
        slot_ref[0] = 1 - slot


def _moe_up_kernel(te_ref, nx_ref, nu_ref, x_ref, w1_hbm, w3_hbm, o_ref, w1s, w3s, wbuf, sem, slot_ref, *, layer):
    i = pl.program_id(1)
    _expert_weights(te_ref, nx_ref, (w1_hbm, w3_hbm), wbuf, sem, slot_ref, (w1s, w3s), layer=layer,
                    col_tile=o_ref.shape[1])

    @pl.when(i < nu_ref[0])
    def _():
        x = x_ref[...]
        g = jnp.dot(x, w1s[...], preferred_element_type=F32)
        u = jnp.dot(x, w3s[...], preferred_element_type=F32)
        o_ref[...] = (_silu(g) * u).astype(o_ref.dtype)

    @pl.when(i >= nu_ref[0])
    def _():
        o_ref[...] = jnp.zeros_like(o_ref)


def _moe_down_kernel(te_ref, nx_ref, nu_ref, h_ref, rw_ref, w2_hbm, o_ref, w2s, wbuf, sem, slot_ref, *, layer):
    i = pl.program_id(1)
    _expert_weights(te_ref, nx_ref, (w2_hbm,), wbuf, sem, slot_ref, (w2s,), layer=layer, col_tile=o_ref.shape[1])

    @pl.when(i < nu_ref[0])
    def _():
        o_ref[...] = jnp.dot(h_ref[...], w2s[...], preferred_element_type=F32) * rw_ref[...]

    @pl.when(i >= nu_ref[0])
    def _():
        o_ref[...] = jnp.zeros_like(o_ref)


def _moe_experts(xs, w1, w3, w2, e, tile_expert, next_expert, n_used, row_w, *, tm, tf, tn):
    P, D = xs.shape
    F = w1.shape[3]
    assert P % tm == 0 and F % tf == 0 and D % tn == 0 and D % _W_CAST_ROWS == 0 and F % _W_CAST_ROWS == 0
    n_tiles = P // tm
    hbm = pl.BlockSpec(memory_space=pl.ANY)
    hmid = pl.pallas_call(
        functools.partial(_moe_up_kernel, layer=e),
        out_shape=jax.ShapeDtypeStruct((P, F), BF16),
        grid_spec=pltpu.PrefetchScalarGridSpec(
            num_scalar_prefetch=3, grid=(F // tf, n_tiles),
            in_specs=[pl.BlockSpec((tm, D), lambda f, i, te, nx, nu: (i, 0)), hbm, hbm],
            out_specs=pl.BlockSpec((tm, tf), lambda f, i, te, nx, nu: (i, f)),
            scratch_shapes=[pltpu.VMEM((D, tf), BF16), pltpu.VMEM((D, tf), BF16),
                            pltpu.VMEM((2, 2, D, tf), F32), pltpu.SemaphoreType.DMA((2,)),
                            pltpu.SMEM((1,), jnp.int32)]),
        compiler_params=_params(("arbitrary", "arbitrary")),
        name="moe_up",
    )(tile_expert, next_expert, n_used, xs, w1, w3)
    return pl.pallas_call(
        functools.partial(_moe_down_kernel, layer=e),
        out_shape=jax.ShapeDtypeStruct((P, D), F32),
        grid_spec=pltpu.PrefetchScalarGridSpec(
            num_scalar_prefetch=3, grid=(D // tn, n_tiles),
            in_specs=[pl.BlockSpec((tm, F), lambda n, i, te, nx, nu: (i, 0)),
                      pl.BlockSpec((tm, 1), lambda n, i, te, nx, nu: (i, 0)), hbm],
            out_specs=pl.BlockSpec((tm, tn), lambda n, i, te, nx, nu: (i, n)),
            scratch_shapes=[pltpu.VMEM((F, tn), BF16), pltpu.VMEM((2, 1, F, tn), F32),
                            pltpu.SemaphoreType.DMA((2,)), pltpu.SMEM((1,), jnp.int32)]),
        compiler_params=_params(("arbitrary", "arbitrary")),
        name="moe_down",
    )(tile_expert, next_expert, n_used, hmid, row_w, w2)


def _pack_bf16_pairs(h):
    w = h.shape[1] // 2
    hb = h.astype(BF16).astype(F32)
    hi = lax.bitcast_convert_type(hb[:, :w], jnp.uint32)
    lo = lax.bitcast_convert_type(hb[:, w:], jnp.uint32)
    return hi | (lo >> 16)


def _unpack_bf16_pairs(u):
    hi = lax.bitcast_convert_type(u & jnp.uint32(0xFFFF0000), F32).astype(BF16)
    lo = lax.bitcast_convert_type(u << 16, F32).astype(BF16)
    return hi, lo


_DMA_UNROLL = 8


def _wait_rows(src_hbm, dst, sem):
    pltpu.make_async_copy(src_hbm.at[pl.ds(0, dst.shape[0]), :], dst, sem).wait()


def _gather_kernel(src_ref, h_hbm, o_ref, buf, sem, *, rows):
    i = pl.program_id(0)
    n = pl.num_programs(0)

    def issue(tile):
        slot = tile % 2

        def body(r, carry):
            pltpu.make_async_copy(h_hbm.at[pl.ds(src_ref[tile * rows + r], 1), :],
                                  buf.at[slot, pl.ds(r, 1), :], sem.at[slot]).start()
            return carry

        lax.fori_loop(0, rows, body, 0, unroll=_DMA_UNROLL)

    @pl.when(i == 0)
    def _():
        issue(i)

    @pl.when(i + 1 < n)
    def _():
        issue(i + 1)

    slot = i % 2
    _wait_rows(h_hbm, buf.at[slot], sem.at[slot])
    w = buf.shape[2]
    hi, lo = _unpack_bf16_pairs(buf[slot])
    o_ref[:, :w] = hi
    o_ref[:, w:] = lo


def _gather_rows(h_packed, row_src, n_rows):
    T, W = h_packed.shape
    rows = 256
    return pl.pallas_call(
        functools.partial(_gather_kernel, rows=rows),
        out_shape=jax.ShapeDtypeStruct((n_rows, 2 * W), BF16),
        grid_spec=pltpu.PrefetchScalarGridSpec(
            num_scalar_prefetch=1, grid=(n_rows // rows,),
            in_specs=[pl.BlockSpec(memory_space=pl.ANY)],
            out_specs=pl.BlockSpec((rows, 2 * W), lambda i, src: (i, 0)),
            scratch_shapes=[pltpu.VMEM((2, rows, W), jnp.uint32), pltpu.SemaphoreType.DMA((2,))]),
        compiler_params=_params(("arbitrary",)),
        name="moe_gather",
    )(row_src, h_packed)


def _combine_kernel(pos_ref, x_ref, gt_ref, fg_ref, ys_hbm, o_ref, buf, sem, *, rows, top_k):
    i = pl.program_id(0)
    n = pl.num_programs(0)

    def issue(tile):
        slot = tile % 2

        def body(r, carry):
            for s in range(top_k):
                pltpu.make_async_copy(ys_hbm.at[pl.ds(pos_ref[(tile * rows + r) * top_k + s], 1), :],
                                      buf.at[slot, s, pl.ds(r, 1), :], sem.at[slot]).start()
            return carry

        lax.fori_loop(0, rows, body, 0, unroll=_DMA_UNROLL)

    @pl.when(i == 0)
    def _():
        issue(i)

    @pl.when(i + 1 < n)
    def _():
        issue(i + 1)

    slot = i % 2
    for s in range(top_k):
        _wait_rows(ys_hbm, buf.at[slot, s], sem.at[slot])
    y = buf[slot, 0]
    for s in range(1, top_k):
        y = y + buf[slot, s]
    x = x_ref[...] + gt_ref[0] * y
    o_ref[...] = _rms(x, fg_ref[...])


def _combine(pos, x, gate, final_g, ys, seq, top_k):
    T, D = x.shape
    B = gate.shape[0]
    rows = 256
    return pl.pallas_call(
        functools.partial(_combine_kernel, rows=rows, top_k=top_k),
        out_shape=jax.ShapeDtypeStruct((T, D), F32),
        grid_spec=pltpu.PrefetchScalarGridSpec(
            num_scalar_prefetch=1, grid=(T // rows,),
            in_specs=[pl.BlockSpec((rows, D), lambda i, p: (i, 0)),
                      pl.BlockSpec((1, 1, D), lambda i, p: (i * rows // seq, 0, 0)),
                      pl.BlockSpec((1, D), lambda i, p: (0, 0)),
                      pl.BlockSpec(memory_space=pl.ANY)],
            out_specs=pl.BlockSpec((rows, D), lambda i, p: (i, 0)),
            scratch_shapes=[pltpu.VMEM((2, top_k, rows, D), F32), pltpu.SemaphoreType.DMA((2,))]),
        compiler_params=_params(("arbitrary",)),
        name="moe_combine",
    )(pos, x, gate.reshape(B, 1, D), final_g.reshape(1, D), ys)


def _route(top_i, top_w, tm, n_tiles):
    T, top_k = top_i.shape
    flat_e = top_i.reshape(-1)
    onehot = (flat_e[:, None] == jnp.arange(N_EXPERTS, dtype=jnp.int32)[None, :]).astype(jnp.int32)
    rank = jnp.sum((jnp.cumsum(onehot, axis=0) - onehot) * onehot, axis=1)
    counts = jnp.sum(onehot, axis=0)
    padded = (counts + tm - 1) // tm * tm
    ends = jnp.cumsum(padded)
    starts = ends - padded
    pos = starts[flat_e] + rank
    token = jnp.arange(T * top_k, dtype=jnp.int32) // top_k
    row_src = jnp.zeros((n_tiles * tm,), jnp.int32).at[pos].set(token)
    row_w = jnp.zeros((n_tiles * tm,), F32).at[pos].set(top_w.reshape(-1))
    tile_start = jnp.arange(n_tiles, dtype=jnp.int32) * tm
    tile_expert = jnp.minimum(jnp.sum((tile_start[:, None] >= ends[None, :]).astype(jnp.int32), axis=1),
                              N_EXPERTS - 1)
    n_used = (ends[-1] // tm).reshape(1)
    tile_id = jnp.arange(n_tiles, dtype=jnp.int32)
    first = jnp.logical_and(tile_id < n_used[0],
                            jnp.concatenate([jnp.ones((1,), bool), tile_expert[1:] != tile_expert[:-1]]))
    first_pos = jnp.where(first, tile_id, n_tiles)
    later_first = jnp.concatenate([lax.cummin(first_pos, reverse=True)[1:], jnp.full((1,), n_tiles, jnp.int32)])
    next_expert = jnp.where(later_first < n_tiles, tile_expert[jnp.minimum(later_first, n_tiles - 1)], -1)
    return (pos.astype(jnp.int32), row_src, row_w.reshape(-1, 1), tile_expert.astype(jnp.int32),
            next_expert.astype(jnp.int32), n_used.astype(jnp.int32))


def _q_up_weight(w):
    r = w.shape[0]
    w = w.reshape(r, MLA_HEADS, MLA_QK)
    nope, rp = w[:, :, :MLA_NOPE], w[:, :, MLA_NOPE:]
    half = MLA_ROPE // 2
    rot = jnp.concatenate([-rp[:, :, half:], rp[:, :, :half]], axis=2)
    z = jnp.zeros((r, MLA_HEADS, V7X_LANES - MLA_ROPE), w.dtype)
    return jnp.concatenate([nope, rp, z, rot, z], axis=2).reshape(r, -1).astype(BF16)


def _kv_up_weight(w):
    r = w.shape[0]
    w = w.reshape(r, MLA_HEADS, MLA_NOPE + MLA_V)
    return jnp.concatenate([w[:, :, :MLA_NOPE].reshape(r, -1), w[:, :, MLA_NOPE:].reshape(r, -1)],
                           axis=1).astype(BF16)


def _rope_tables(positions):
    pos = positions.reshape(-1).astype(F32)[:, None]
    inv_m = ROPE_BASE ** (-jnp.arange(0, MLA_ROPE, 2, dtype=F32) / MLA_ROPE)
    am = pos * inv_m
    zm = jnp.zeros((pos.shape[0], V7X_LANES - MLA_ROPE), F32)
    cos_m = jnp.concatenate([jnp.cos(am), jnp.cos(am), zm], axis=1)
    sin_m = jnp.concatenate([jnp.sin(am), jnp.sin(am), zm], axis=1)
    inv_r = ROPE_BASE ** (-jnp.arange(0, RET_DK, 2, dtype=F32) / RET_DK)
    ar = pos * inv_r
    cos_r = jnp.concatenate([jnp.cos(ar), jnp.cos(ar)], axis=1)
    sin_r = jnp.concatenate([-jnp.sin(ar), jnp.sin(ar)], axis=1)
    return cos_m, sin_m, cos_r, sin_r


def kernel(x, c, positions, w_ada, b_ada, ada_table, norm_g, w_in, mla_q_norm, mla_w_q_up, mla_kv_norm,
           mla_w_kv_up, hg_lb_logits, hg_norm, ret_norm, w_branch, w_out, ffn_w1, ffn_w3, ffn_w2,
           moe_router, moe_w1, moe_w3, moe_w2, final_norm):
    B, S, D = x.shape
    T = B * S
    depth = norm_g.shape[0]
    top_k = 2
    assert depth % 2 == 0, "the final norm is fused into the expert combine of the last (odd) layer"
    assert S % 512 == 0 and D % 512 == 0

    mod = _ada(c, w_ada, b_ada, ada_table)
    cos_m, sin_m, cos_r, sin_r = _rope_tables(positions)

    n_cqkv = MLA_Q_RANK + MLA_KV_RANK
    n_lat = n_cqkv + MLA_ROPE
    n_hg = 4 * HG_WIDTH
    n_ret = 2 * RET_HEADS * RET_DK + 2 * RET_HEADS * RET_DV

    d_in = w_in.shape[2]
    w_in_t = jnp.swapaxes(w_in, 1, 2).reshape(depth, d_in // _WT_GROUP, _WT_GROUP, D)

    xf = x.reshape(T, D)
    pending = None
    out = None
    for l in range(depth):
        shift1, scale1, gate1, shift2, scale2, gate2 = (mod[l, :, j] for j in range(N_MOD))
        if pending is None:
            (h,) = _norm(xf, norm_g[l, 0], scale1, shift1, S)
        else:
            xf, h = _norm(xf, norm_g[l, 0], scale1, shift1, S, resid=pending)
            pending = None

        lat = _matmul(h, w_in_t, l, 0, n_cqkv, transposed=True)
        kr_raw = _matmul(h, w_in_t, l, n_cqkv, V7X_LANES, transposed=True, tn=V7X_LANES)
        hg = _matmul(h, w_in_t, l, n_lat, n_hg, transposed=True)
        rt = _matmul(h, w_in_t, l, n_lat + n_hg, n_ret, transposed=True)
        gates = _matmul(h, w_in_t, l, n_lat + n_hg + n_ret, N_BRANCH * D, transposed=True, mode="sigmoid",
                        out_dtype=BF16)

        y_mla = _mla(lat, kr_raw, cos_m, sin_m, mla_q_norm[l], _q_up_weight(mla_w_q_up[l]),
                     mla_kv_norm[l], _kv_up_weight(mla_w_kv_up[l]), B, S)
        y_hg = _hgrn2(hg, hg_lb_logits, hg_norm[l], l, B, S)
        y_ret = _retention(rt, cos_r, sin_r, ret_norm[l], B, S)

        merged = _merge(y_mla, y_hg, y_ret, w_branch, l, gates)
        xf = _matmul(merged, w_out, l, 0, D, mode="resid", resid=xf, gate=gate1, seq=S)

        if l % 2 == 0:
            (h2,) = _norm(xf, norm_g[l, 1], scale2, shift2, S)
            e = l // 2
            y = _ffn(h2, ffn_w1.astype(BF16), ffn_w3.astype(BF16), ffn_w2.astype(BF16), e, tm=512, tf=256)
            pending = (y, gate2)
        else:
            e = l // 2
            h2, ti, tw = _norm(xf, norm_g[l, 1], scale2, shift2, S, router=moe_router[e])
            tm = 512
            n_tiles = T * top_k // tm + N_EXPERTS
            pos, row_src, row_w, tile_expert, next_expert, n_used = _route(ti[:, :top_k], tw[:, :top_k], tm,
                                                                           n_tiles)
            xs = _gather_rows(h2, row_src, n_tiles * tm)
            ys = _moe_experts(xs, moe_w1, moe_w3, moe_w2, e, tile_expert, next_expert, n_used, row_w, tm=tm,
                              tf=512, tn=1024)
            assert l == depth - 1
            out = _combine(pos, xf, gate2, final_norm, ys, S, top_k)
    return out.reshape(B, S, D)
```

```python
import functools
import math

import jax
import jax.numpy as jnp
from jax import lax
from jax.experimental import pallas as pl
from jax.experimental.pallas import tpu as pltpu

F32 = jnp.float32
BF16 = jnp.bfloat16

V7X_LANES = 128
V7X_SUBLANES = 8
V7X_VMEM_BYTES = 64 * 1024 * 1024
V7X_VMEM_LIMIT = V7X_VMEM_BYTES - 8 * 1024 * 1024

EPS = 1e-6
ROPE_BASE = 10000.0
MASK_VALUE = -1e30

MLA_HEADS = 16
MLA_NOPE = 128
MLA_ROPE = 64
MLA_V = 128
MLA_Q_RANK = 1024
MLA_KV_RANK = 512
MLA_QK = MLA_NOPE + MLA_ROPE
MLA_QPAD = 2 * V7X_LANES

HG_HEADS = 16
HG_D = 128
HG_WIDTH = HG_HEADS * HG_D
HG_CHUNK = 64
HG_SUB = 8
HG_MIN_F = 1e-6

RET_HEADS = 8
RET_DK = 128
RET_DV = 256
RET_CHUNK = 128

N_BRANCH = 3
N_EXPERTS = 8
N_MOD = 6


def _params(semantics):
    return pltpu.CompilerParams(dimension_semantics=semantics, vmem_limit_bytes=V7X_VMEM_LIMIT)


def _rms(x, gain):
    return x * lax.rsqrt(jnp.mean(x * x, axis=-1, keepdims=True) + EPS) * gain


def _silu(x):
    return x * jax.nn.sigmoid(x)


def _dot_nt(a, b):
    return lax.dot_general(a, b, (((1,), (1,)), ((), ())), preferred_element_type=F32)


def _dot_tn(a, b):
    return jnp.dot(a.T, b, preferred_element_type=F32)


def _ada_kernel(c_ref, w_ref, b_ref, tab_ref, o_ref, acc_ref):
    k = pl.program_id(0)

    @pl.when(k == 0)
    def _():
        acc_ref[...] = jnp.zeros_like(acc_ref)

    s = _silu(c_ref[...]).astype(BF16)
    acc_ref[...] += jnp.dot(s, w_ref[...].astype(BF16), preferred_element_type=F32)

    @pl.when(k == pl.num_programs(0) - 1)
    def _():
        acc = acc_ref[...] + b_ref[...]
        for l in range(o_ref.shape[0]):
            o_ref[l] = acc + tab_ref[l]


def _ada(c, w_ada, b_ada, ada_table):
    B, D = c.shape
    depth = ada_table.shape[0]
    N = w_ada.shape[1]
    tk = V7X_LANES
    cp = jnp.zeros((V7X_SUBLANES, D), F32).at[:B].set(c)
    out = pl.pallas_call(
        _ada_kernel,
        out_shape=jax.ShapeDtypeStruct((depth, V7X_SUBLANES, N), F32),
        grid=(D // tk,),
        in_specs=[
            pl.BlockSpec((V7X_SUBLANES, tk), lambda k: (0, k)),
            pl.BlockSpec((tk, N), lambda k: (k, 0)),
            pl.BlockSpec((1, N), lambda k: (0, 0)),
            pl.BlockSpec((depth, 1, N), lambda k: (0, 0, 0)),
        ],
        out_specs=pl.BlockSpec((depth, V7X_SUBLANES, N), lambda k: (0, 0, 0)),
        scratch_shapes=[pltpu.VMEM((V7X_SUBLANES, N), F32)],
        compiler_params=_params(("arbitrary",)),
        name="ada_mod",
    )(cp, w_ada, b_ada.reshape(1, N), ada_table.reshape(depth, 1, N))
    return out[:, :B].reshape(depth, B, N_MOD, D)


def _norm_kernel(*refs, has_resid, has_router, n_experts):
    it = iter(refs)
    x_ref = next(it)
    if has_resid:
        y_ref, gt_ref = next(it), next(it)
    g_ref, sc_ref, sh_ref = next(it), next(it), next(it)
    if has_router:
        r_ref = next(it)
    if has_resid:
        xo_ref = next(it)
    h_ref = next(it)
    if has_router:
        ti_ref, tw_ref = next(it), next(it)

    x = x_ref[...]
    if has_resid:
        x = x + gt_ref[0] * y_ref[...]
        xo_ref[...] = x
    h = _rms(x, g_ref[...]) * (1.0 + sc_ref[0]) + sh_ref[0]
    if not has_router:
        h_ref[...] = h.astype(h_ref.dtype)
    else:
        h_ref[...] = _pack_bf16_pairs(h)
        logits = jnp.dot(h, r_ref[...], preferred_element_type=F32, precision=lax.Precision.HIGHEST)
        lane = lax.broadcasted_iota(jnp.int32, logits.shape, 1).astype(F32)
        neg = jnp.float32(-jnp.inf)
        big = jnp.float32(V7X_LANES)
        lg = jnp.where(lane < n_experts, logits, neg)
        m1 = jnp.max(lg, axis=-1, keepdims=True)
        i1 = jnp.min(jnp.where(lg == m1, lane, big), axis=-1, keepdims=True)
        lg2 = jnp.where(lane == i1, neg, lg)
        m2 = jnp.max(lg2, axis=-1, keepdims=True)
        i2 = jnp.min(jnp.where(lg2 == m2, lane, big), axis=-1, keepdims=True)
        e = jnp.exp(m2 - m1)
        w1 = 1.0 / (1.0 + e)
        w2 = e / (1.0 + e)
        ti_ref[...] = jnp.where(lane == 0.0, i1, jnp.where(lane == 1.0, i2, 0.0)).astype(jnp.int32)
        tw_ref[...] = jnp.where(lane == 0.0, w1, jnp.where(lane == 1.0, w2, 0.0))


def _norm(x, gain, scale, shift, seq, *, resid=None, router=None):
    T, D = x.shape
    B = scale.shape[0]
    tm = 256
    row = pl.BlockSpec((tm, D), lambda i: (i, 0))
    per_batch = pl.BlockSpec((1, 1, D), lambda i: (i * tm // seq, 0, 0))
    args, in_specs = [x], [row]
    if resid is not None:
        args += [resid[0], resid[1].reshape(B, 1, D)]
        in_specs += [row, per_batch]
    args += [gain.reshape(1, D), scale.reshape(B, 1, D), shift.reshape(B, 1, D)]
    in_specs += [pl.BlockSpec((1, D), lambda i: (0, 0)), per_batch, per_batch]
    out_shape, out_specs = [], []
    if router is not None:
        n_experts = router.shape[1]
        rp = jnp.zeros((D, V7X_LANES), F32).at[:, :n_experts].set(router)
        args.append(rp)
        in_specs.append(pl.BlockSpec((D, V7X_LANES), lambda i: (0, 0)))
    else:
        n_experts = 0
    if resid is not None:
        out_shape.append(jax.ShapeDtypeStruct((T, D), F32))
        out_specs.append(row)
    if router is None:
        out_shape.append(jax.ShapeDtypeStruct((T, D), BF16))
        out_specs.append(row)
    else:
        out_shape.append(jax.ShapeDtypeStruct((T, D // 2), jnp.uint32))
        out_specs.append(pl.BlockSpec((tm, D // 2), lambda i: (i, 0)))
    if router is not None:
        small = pl.BlockSpec((tm, V7X_LANES), lambda i: (i, 0))
        out_shape += [jax.ShapeDtypeStruct((T, V7X_LANES), jnp.int32), jax.ShapeDtypeStruct((T, V7X_LANES), F32)]
        out_specs += [small, small]
    return pl.pallas_call(
        functools.partial(_norm_kernel, has_resid=resid is not None, has_router=router is not None,
                          n_experts=n_experts),
        out_shape=out_shape,
        grid=(T // tm,),
        in_specs=in_specs,
        out_specs=out_specs,
        compiler_params=_params(("arbitrary",)),
        name="ada_norm",
    )(*args)


_W_CAST_ROWS = 512


_WT_GROUP = 64


def _mm_kernel(*refs, mode, n_groups):
    it = iter(refs)
    a_ref = next(it)
    w_refs = [next(it) for _ in range(max(n_groups, 1))]
    if mode == "resid":
        r_ref, g_ref = next(it), next(it)
    o_ref, wscr = next(it), next(it)
    c = pl.program_id(2)

    @pl.when(pl.program_id(1) == 0)
    def _():
        if n_groups:
            for g, w_ref in enumerate(w_refs):
                wscr[c, g * _WT_GROUP:(g + 1) * _WT_GROUP, :] = w_ref[0, 0].astype(BF16)
        else:
            for r0 in range(0, wscr.shape[1], _W_CAST_ROWS):
                wscr[c, r0:r0 + _W_CAST_ROWS, :] = w_refs[0][0, r0:r0 + _W_CAST_ROWS, :].astype(BF16)

    if n_groups:
        acc = _dot_nt(a_ref[...], wscr[c])
    else:
        acc = jnp.dot(a_ref[...], wscr[c], preferred_element_type=F32)
    if mode == "plain":
        o_ref[...] = acc.astype(o_ref.dtype)
    elif mode == "sigmoid":
        o_ref[...] = jax.nn.sigmoid(acc).astype(o_ref.dtype)
    else:
        o_ref[...] = r_ref[...] + g_ref[0] * acc


def _matmul(a, w, layer, col0, n, *, transposed=False, mode="plain", out_dtype=F32, resid=None, gate=None,
            seq=None, tm=1024, tn=512):
    M, K = a.shape
    assert n % tn == 0 and M % tm == 0
    pair = 2 if (n // tn) % 2 == 0 else 1

    def col_tile(jj, i, c):
        return jj * pair + jnp.where(i == 0, c, pair - 1)

    args = [a]
    in_specs = [pl.BlockSpec((tm, K), lambda jj, i, c: (i, 0))]
    if transposed:
        assert col0 % _WT_GROUP == 0 and tn % _WT_GROUP == 0
        n_groups = tn // _WT_GROUP
        g0 = col0 // _WT_GROUP
        for g in range(n_groups):
            args.append(w)
            in_specs.append(pl.BlockSpec(
                (1, 1, _WT_GROUP, K),
                lambda jj, i, c, g=g: (layer, g0 + col_tile(jj, i, c) * n_groups + g, 0, 0)))
        wscr = pltpu.VMEM((pair, tn, K), BF16)
    else:
        assert col0 % tn == 0 and K % _W_CAST_ROWS == 0
        n_groups = 0
        args.append(w)
        in_specs.append(pl.BlockSpec((1, K, tn), lambda jj, i, c: (layer, 0, col0 // tn + col_tile(jj, i, c))))
        wscr = pltpu.VMEM((pair, K, tn), BF16)
    out_spec = pl.BlockSpec((tm, tn), lambda jj, i, c: (i, jj * pair + c))
    if mode == "resid":
        B = gate.shape[0]
        args += [resid, gate.reshape(B, 1, n)]
        in_specs += [out_spec, pl.BlockSpec((1, 1, tn), lambda jj, i, c: (i * tm // seq, 0, jj * pair + c))]
    return pl.pallas_call(
        functools.partial(_mm_kernel, mode=mode, n_groups=n_groups),
        out_shape=jax.ShapeDtypeStruct((M, n), out_dtype),
        grid=(n // (tn * pair), M // tm, pair),
        in_specs=in_specs,
        out_specs=out_spec,
        scratch_shapes=[wscr],
        compiler_params=_params(("arbitrary", "arbitrary", "arbitrary")),
        name="matmul_" + mode,
    )(*args)


def _mla_q_kernel(cq_ref, g_ref, w_ref, cos_ref, sin_ref, o_ref, a_scr, *, heads_per_step, scale):
    @pl.when(pl.program_id(1) == 0)
    def _():
        a_scr[...] = _rms(cq_ref[...], g_ref[...]).astype(BF16)

    acc = jnp.dot(a_scr[...], w_ref[...], preferred_element_type=F32)
    cos, sin = cos_ref[...], sin_ref[...]
    L = V7X_LANES
    for h in range(heads_per_step):
        nope = acc[:, 2 * L * h:2 * L * h + L]
        rp = acc[:, 2 * L * h + L:2 * L * h + 2 * L]
        o_ref[:, 2 * L * h:2 * L * h + L] = (nope * scale).astype(BF16)
        roped = rp * cos + pltpu.roll(rp, MLA_ROPE, 1) * sin
        o_ref[:, 2 * L * h + L:2 * L * h + 2 * L] = (roped * scale).astype(BF16)


def _mla_kv_kernel(ckv_ref, g_ref, w_ref, kraw_ref, cos_ref, sin_ref, kv_ref, kr_ref, a_scr):
    @pl.when(pl.program_id(1) == 0)
    def _():
        a_scr[...] = _rms(ckv_ref[...], g_ref[...]).astype(BF16)
        x = kraw_ref[...]
        half = MLA_ROPE // 2
        lane = lax.broadcasted_iota(jnp.int32, x.shape, 1)
        swapped = jnp.where(lane < half, -pltpu.roll(x, V7X_LANES - half, 1), pltpu.roll(x, half, 1))
        roped = x * cos_ref[...] + swapped * sin_ref[...]
        kr_ref[...] = jnp.where(lane < MLA_ROPE, roped, 0.0).astype(BF16)

    kv_ref[...] = jnp.dot(a_scr[...], w_ref[...], preferred_element_type=F32).astype(BF16)


def _mla_attn_kernel(q_ref, k_ref, kr_ref, v_ref, o_ref, *, tq, n_heads):
    i = pl.program_id(2)
    qs = [q_ref[:, h * MLA_QPAD:(h + 1) * MLA_QPAD] for h in range(n_heads)]

    ones = jnp.ones((tq, V7X_LANES), BF16)

    def block(j, carry, masked):
        off = pl.multiple_of(j * tq, tq)
        kr = kr_ref[pl.ds(off, tq), :]
        scores = []
        for h in range(n_heads):
            kk = jnp.concatenate([k_ref[pl.ds(off, tq), h * MLA_NOPE:(h + 1) * MLA_NOPE], kr], axis=1)
            scores.append(_dot_nt(qs[h], kk))
        out = []
        for h in range(n_heads):
            m, acc = carry[h]
            s = scores[h]
            if masked:
                r = lax.broadcasted_iota(jnp.int32, s.shape, 0)
                c = lax.broadcasted_iota(jnp.int32, s.shape, 1)
                s = jnp.where(c <= r, s, MASK_VALUE)
            m_new = jnp.maximum(m, jnp.max(s, axis=-1, keepdims=True))
            p = jnp.exp(s - m_new).astype(BF16)
            v1 = jnp.concatenate([v_ref[pl.ds(off, tq), h * MLA_V:(h + 1) * MLA_V], ones], axis=1)
            acc = jnp.exp(m - m_new) * acc + jnp.dot(p, v1, preferred_element_type=F32)
            out.append((m_new, acc))
        return tuple(out)

    carry = tuple((jnp.full((tq, 1), MASK_VALUE, F32), jnp.zeros((tq, MLA_V + V7X_LANES), F32))
                  for _ in range(n_heads))
    carry = lax.fori_loop(0, i, lambda j, c: block(j, c, False), carry)
    carry = block(i, carry, True)
    for h in range(n_heads):
        _, acc = carry[h]
        o_ref[:, h * MLA_V:(h + 1) * MLA_V] = (acc[:, :MLA_V] / acc[:, MLA_V:]).astype(o_ref.dtype)


def _mla(lat, kr_raw, cos_m, sin_m, q_norm, wq, kv_norm, wkv, B, S):
    T = lat.shape[0]
    L = V7X_LANES
    tm = 512
    hps = 4
    scale = MLA_QK ** -0.5
    q = pl.pallas_call(
        functools.partial(_mla_q_kernel, heads_per_step=hps, scale=scale),
        out_shape=jax.ShapeDtypeStruct((T, MLA_HEADS * MLA_QPAD), BF16),
        grid=(T // tm, MLA_HEADS // hps),
        in_specs=[
            pl.BlockSpec((tm, MLA_Q_RANK), lambda i, j: (i, 0)),
            pl.BlockSpec((1, MLA_Q_RANK), lambda i, j: (0, 0)),
            pl.BlockSpec((MLA_Q_RANK, hps * MLA_QPAD), lambda i, j: (0, j)),
            pl.BlockSpec((tm, L), lambda i, j: (i, 0)),
            pl.BlockSpec((tm, L), lambda i, j: (i, 0)),
        ],
        out_specs=pl.BlockSpec((tm, hps * MLA_QPAD), lambda i, j: (i, j)),
        scratch_shapes=[pltpu.VMEM((tm, MLA_Q_RANK), BF16)],
        compiler_params=_params(("arbitrary", "arbitrary")),
        name="mla_q",
    )(lat, q_norm.reshape(1, -1), wq, cos_m, sin_m)

    tn = 1024
    kv_w = 2 * MLA_HEADS * MLA_NOPE
    ckv_blk = MLA_Q_RANK // MLA_KV_RANK
    kv, kr = pl.pallas_call(
        _mla_kv_kernel,
        out_shape=[jax.ShapeDtypeStruct((T, kv_w), BF16), jax.ShapeDtypeStruct((T, L), BF16)],
        grid=(T // tm, kv_w // tn),
        in_specs=[
            pl.BlockSpec((tm, MLA_KV_RANK), lambda i, j: (i, ckv_blk)),
            pl.BlockSpec((1, MLA_KV_RANK), lambda i, j: (0, 0)),
            pl.BlockSpec((MLA_KV_RANK, tn), lambda i, j: (0, j)),
            pl.BlockSpec((tm, L), lambda i, j: (i, 0)),
            pl.BlockSpec((tm, L), lambda i, j: (i, 0)),
            pl.BlockSpec((tm, L), lambda i, j: (i, 0)),
        ],
        out_specs=[pl.BlockSpec((tm, tn), lambda i, j: (i, j)), pl.BlockSpec((tm, L), lambda i, j: (i, 0))],
        scratch_shapes=[pltpu.VMEM((tm, MLA_KV_RANK), BF16)],
        compiler_params=_params(("arbitrary", "arbitrary")),
        name="mla_kv",
    )(lat, kv_norm.reshape(1, -1), wkv, kr_raw, cos_m, sin_m)

    tq = 512
    nq = S // tq
    nh = 4
    hg = MLA_HEADS // nh
    return pl.pallas_call(
        functools.partial(_mla_attn_kernel, tq=tq, n_heads=nh),
        out_shape=jax.ShapeDtypeStruct((T, MLA_HEADS * MLA_V), BF16),
        grid=(B, hg, nq),
        in_specs=[
            pl.BlockSpec((tq, nh * MLA_QPAD), lambda b, h, i: (b * nq + i, h)),
            pl.BlockSpec((S, nh * MLA_NOPE), lambda b, h, i: (b, h)),
            pl.BlockSpec((S, L), lambda b, h, i: (b, 0)),
            pl.BlockSpec((S, nh * MLA_V), lambda b, h, i: (b, hg + h)),
        ],
        out_specs=pl.BlockSpec((tq, nh * MLA_V), lambda b, h, i: (b * nq + i, h)),
        compiler_params=_params(("arbitrary", "arbitrary", "arbitrary")),
        name="mla_attn",
    )(q, kv, kr, kv)


def _hgrn2_kernel(q_ref, f_ref, i_ref, gate_ref, lbl_ref, ng_ref, o_ref, st_scr, q_scr, k_scr, b_scr, *, layer):
    tb = q_ref.shape[0]
    C = HG_CHUNK

    @pl.when(pl.program_id(2) == 0)
    def _():
        st_scr[...] = jnp.zeros_like(st_scr)

    lg = lbl_ref[...]
    e = jnp.exp(lg - jnp.max(lg, axis=0, keepdims=True))
    p = e / jnp.sum(e, axis=0, keepdims=True)
    lb = jnp.zeros((1, HG_D), F32)
    for l in range(1, layer + 1):
        lb = lb + p[l:l + 1, :]

    fr = f_ref[...]
    f = lb + (1.0 - lb) * jax.nn.sigmoid(fr)
    logf = jnp.log(jnp.maximum(f, HG_MIN_F))
    q_scr[...] = _silu(q_ref[...])
    k_scr[...] = (1.0 - lb) * jax.nn.sigmoid(-fr)

    hi = logf.astype(BF16)
    r1 = logf - hi.astype(F32)
    mid = r1.astype(BF16)
    lo = (r1 - mid.astype(F32)).astype(BF16)
    rr = lax.broadcasted_iota(jnp.int32, (C, C), 0)
    cc = lax.broadcasted_iota(jnp.int32, (C, C), 1)
    causal = cc <= rr
    tril = jnp.where(causal, 1.0, 0.0).astype(BF16)
    for c in range(tb // C):
        sl = slice(c * C, (c + 1) * C)
        parts = jnp.concatenate([hi[sl], mid[sl], lo[sl]], axis=1)
        cs = jnp.dot(tril, parts, preferred_element_type=F32)
        b_scr[sl, :] = cs[:, :HG_D] + cs[:, HG_D:2 * HG_D] + cs[:, 2 * HG_D:]

    intra, incr = [], []
    for c in range(tb // C):
        r0 = c * C
        b_c = b_scr[r0:r0 + C, :]
        k_c = k_scr[r0:r0 + C, :]
        v_c = i_ref[r0:r0 + C, :]
        b_last = b_scr[r0 + C - 1:r0 + C, :]
        rows = []
        for blk in range(C // HG_SUB):
            s0 = r0 + blk * HG_SUB
            n = (blk + 1) * HG_SUB
            ref = b_scr[s0 + HG_SUB // 2 - 1:s0 + HG_SUB // 2, :]
            q_b = q_scr[s0:s0 + HG_SUB, :] * jnp.exp(b_scr[s0:s0 + HG_SUB, :] - ref)
            k_b = k_scr[r0:r0 + n, :] * jnp.exp(ref - b_scr[r0:r0 + n, :])
            if n < C:
                k_b = jnp.concatenate([k_b, jnp.zeros((C - n, HG_D), F32)], axis=0)
            rows.append(_dot_nt(q_b, k_b))
        scores = jnp.where(causal, jnp.concatenate(rows, axis=0), 0.0)
        intra.append(jnp.dot(scores, v_c, preferred_element_type=F32))
        incr.append(_dot_tn(v_c, k_c * jnp.exp(b_last - b_c)))

    st = st_scr[...]
    ng = ng_ref[...]
    for c in range(tb // C):
        r0 = c * C
        b_c = b_scr[r0:r0 + C, :]
        b_last = b_scr[r0 + C - 1:r0 + C, :]
        o = intra[c] + _dot_nt(q_scr[r0:r0 + C, :] * jnp.exp(b_c), st)
        st = st * jnp.exp(b_last) + incr[c]
        y = _rms(o, ng) * _silu(gate_ref[r0:r0 + C, :])
        o_ref[r0:r0 + C, :] = y.astype(o_ref.dtype)
    st_scr[...] = st


def _hgrn2(hg, lb_logits, norm_g, layer, B, S):
    T = hg.shape[0]
    tb = 512
    nb = S // tb
    depth = lb_logits.shape[0]

    def part(p):
        return pl.BlockSpec((tb, HG_D), lambda b, h, j: (b * nb + j, p * HG_HEADS + h))

    return pl.pallas_call(
        functools.partial(_hgrn2_kernel, layer=layer),
        out_shape=jax.ShapeDtypeStruct((T, HG_WIDTH), BF16),
        grid=(B, HG_HEADS, nb),
        in_specs=[part(0), part(1), part(2), part(3),
                  pl.BlockSpec((depth, HG_D), lambda b, h, j: (0, h)),
                  pl.BlockSpec((1, HG_D), lambda b, h, j: (0, h))],
        out_specs=pl.BlockSpec((tb, HG_D), lambda b, h, j: (b * nb + j, h)),
        scratch_shapes=[pltpu.VMEM((HG_D, HG_D), F32), pltpu.VMEM((tb, HG_D), F32),
                        pltpu.VMEM((tb, HG_D), F32), pltpu.VMEM((tb, HG_D), F32)],
        compiler_params=_params(("arbitrary", "arbitrary", "arbitrary")),
        name="hgrn2",
    )(hg, hg, hg, hg, lb_logits, norm_g.reshape(1, -1))


def _ret_kernel(q_ref, k_ref, v_ref, gate_ref, cos_ref, sin_ref, lg_ref, ng_ref, o_ref, st_scr):
    tb = q_ref.shape[0]
    C = RET_CHUNK

    @pl.when(pl.program_id(2) == 0)
    def _():
        st_scr[...] = jnp.zeros_like(st_scr)

    cos, sin = cos_ref[...], sin_ref[...]
    half = RET_DK // 2
    xq, xk = q_ref[...], k_ref[...]
    q = xq * cos + pltpu.roll(xq, half, 1) * sin
    k = (xk * cos + pltpu.roll(xk, half, 1) * sin) * (RET_DK ** -0.5)

    lg = lg_ref[0]
    rr = lax.broadcasted_iota(jnp.int32, (C, C), 0)
    cc = lax.broadcasted_iota(jnp.int32, (C, C), 1)
    rel = (rr - cc).astype(F32)
    decay = jnp.where(rel >= 0.0, jnp.exp(lg * jnp.maximum(rel, 0.0)), 0.0)
    idx = rr.astype(F32)
    q_dec = jnp.exp(lg * (idx + 1.0))
    k_dec = jnp.exp(lg * (C - 1.0 - idx))
    chunk_dec = jnp.exp(lg * float(C))

    intra, incr = [], []
    for c in range(tb // C):
        sl = slice(c * C, (c + 1) * C)
        q_c, k_c = q[sl], k[sl]
        v_c = v_ref[sl, :]
        s = _dot_nt(q_c.astype(BF16), k_c.astype(BF16)) * decay
        intra.append(jnp.dot(s.astype(BF16), v_c.astype(BF16), preferred_element_type=F32))
        incr.append(jnp.dot(v_c.T.astype(BF16), (k_c * k_dec).astype(BF16), preferred_element_type=F32))

    st = st_scr[...]
    ng = ng_ref[...]
    for c in range(tb // C):
        sl = slice(c * C, (c + 1) * C)
        o = intra[c] + _dot_nt((q[sl] * q_dec).astype(BF16), st.astype(BF16))
        st = st * chunk_dec + incr[c]
        y = _rms(o, ng) * _silu(gate_ref[sl, :])
        o_ref[sl, :] = y.astype(o_ref.dtype)
    st_scr[...] = st


def _retention(rt, cos_r, sin_r, norm_g, B, S):
    T = rt.shape[0]
    tb = 512
    nb = S // tb
    H = RET_HEADS
    v_blk0 = 2 * H * RET_DK // RET_DV
    log_gamma = jnp.log1p(-(2.0 ** (-5.0 - jnp.arange(H, dtype=F32))))
    lg = jnp.broadcast_to(log_gamma[:, None, None], (H, 1, RET_CHUNK))
    return pl.pallas_call(
        _ret_kernel,
        out_shape=jax.ShapeDtypeStruct((T, H * RET_DV), BF16),
        grid=(B, H, nb),
        in_specs=[
            pl.BlockSpec((tb, RET_DK), lambda b, h, j: (b * nb + j, h)),
            pl.BlockSpec((tb, RET_DK), lambda b, h, j: (b * nb + j, H + h)),
            pl.BlockSpec((tb, RET_DV), lambda b, h, j: (b * nb + j, v_blk0 + h)),
            pl.BlockSpec((tb, RET_DV), lambda b, h, j: (b * nb + j, v_blk0 + H + h)),
            pl.BlockSpec((tb, RET_DK), lambda b, h, j: (b * nb + j, 0)),
            pl.BlockSpec((tb, RET_DK), lambda b, h, j: (b * nb + j, 0)),
            pl.BlockSpec((1, 1, RET_CHUNK), lambda b, h, j: (h, 0, 0)),
            pl.BlockSpec((1, RET_DV), lambda b, h, j: (0, h)),
        ],
        out_specs=pl.BlockSpec((tb, RET_DV), lambda b, h, j: (b * nb + j, h)),
        scratch_shapes=[pltpu.VMEM((RET_DV, RET_DK), F32)],
        compiler_params=_params(("arbitrary", "arbitrary", "arbitrary")),
        name="retention",
    )(rt, rt, rt, rt, cos_r, sin_r, lg, norm_g.reshape(1, -1))


def _merge_kernel(ya_ref, yb_ref, yc_ref, w_ref, ga_ref, gb_ref, gc_ref, o_ref, wscr):
    @pl.when(pl.program_id(1) == 0)
    def _():
        for br in range(N_BRANCH):
            for r0 in range(0, wscr.shape[1], _W_CAST_ROWS):
                wscr[br, r0:r0 + _W_CAST_ROWS, :] = w_ref[0, br, r0:r0 + _W_CAST_ROWS, :].astype(BF16)

    acc = ga_ref[...].astype(F32) * jnp.dot(ya_ref[...], wscr[0], preferred_element_type=F32)
    acc = acc + gb_ref[...].astype(F32) * jnp.dot(yb_ref[...], wscr[1], preferred_element_type=F32)
    acc = acc + gc_ref[...].astype(F32) * jnp.dot(yc_ref[...], wscr[2], preferred_element_type=F32)
    o_ref[...] = acc.astype(o_ref.dtype)


def _merge(ya, yb, yc, w_branch, layer, gates):
    T, K = ya.shape
    D = w_branch.shape[3]
    tm, tn = 512, 512
    nj = D // tn
    ybs = pl.BlockSpec((tm, K), lambda j, i: (i, 0))
    return pl.pallas_call(
        _merge_kernel,
        out_shape=jax.ShapeDtypeStruct((T, D), BF16),
        grid=(nj, T // tm),
        in_specs=[ybs, ybs, ybs,
                  pl.BlockSpec((1, N_BRANCH, K, tn), lambda j, i: (layer, 0, 0, j)),
                  pl.BlockSpec((tm, tn), lambda j, i: (i, j)),
                  pl.BlockSpec((tm, tn), lambda j, i: (i, nj + j)),
                  pl.BlockSpec((tm, tn), lambda j, i: (i, 2 * nj + j))],
        out_specs=pl.BlockSpec((tm, tn), lambda j, i: (i, j)),
        scratch_shapes=[pltpu.VMEM((N_BRANCH, K, tn), BF16)],
        compiler_params=_params(("arbitrary", "arbitrary")),
        name="merge",
    )(ya, yb, yc, w_branch, gates, gates, gates)


def _ffn_kernel(x_ref, w1_ref, w3_ref, w2_ref, o_ref):
    @pl.when(pl.program_id(1) == 0)
    def _():
        o_ref[...] = jnp.zeros_like(o_ref)

    x = x_ref[...]
    g = jnp.dot(x, w1_ref[0], preferred_element_type=F32)
    u = jnp.dot(x, w3_ref[0], preferred_element_type=F32)
    o_ref[...] += jnp.dot((_silu(g) * u).astype(BF16), w2_ref[0], preferred_element_type=F32)


def _ffn(x, w1, w3, w2, e, *, tm, tf):
    M, D = x.shape
    F = w1.shape[2]
    assert M % tm == 0 and F % tf == 0
    w13 = pl.BlockSpec((1, D, tf), lambda i, f: (e, 0, f))
    return pl.pallas_call(
        _ffn_kernel,
        out_shape=jax.ShapeDtypeStruct((M, D), F32),
        grid=(M // tm, F // tf),
        in_specs=[pl.BlockSpec((tm, D), lambda i, f: (i, 0)), w13, w13,
                  pl.BlockSpec((1, tf, D), lambda i, f: (e, f, 0))],
        out_specs=pl.BlockSpec((tm, D), lambda i, f: (i, 0)),
        compiler_params=_params(("arbitrary", "arbitrary")),
        name="ffn_dense",
    )(x, w1, w3, w2)


def _expert_changed(te_ref, i):
    return jnp.logical_or(i == 0, te_ref[i] != te_ref[jnp.maximum(i - 1, 0)])


def _expert_weights(te_ref, nx_ref, w_hbms, wbuf, sem, slot_ref, dsts, *, layer, col_tile):
    sweep = pl.program_id(0)
    i = pl.program_id(1)

    def copies(expert, swp, slot):
        col = pl.multiple_of(swp * col_tile, col_tile)
        return [pltpu.make_async_copy(w.at[layer, expert, :, pl.ds(col, col_tile)], wbuf.at[slot, k], sem.at[slot])
                for k, w in enumerate(w_hbms)]

    @pl.when(_expert_changed(te_ref, i))
    def _():
        @pl.when(jnp.logical_and(sweep == 0, i == 0))
        def _():
            slot_ref[0] = 0
            for cp in copies(te_ref[0], 0, 0):
                cp.start()

        slot = slot_ref[0]
        for cp in copies(te_ref[i], sweep, slot):
            cp.wait()
        nxt = nx_ref[i]

        @pl.when(nxt >= 0)
        def _():
            for cp in copies(nxt, sweep, 1 - slot):
                cp.start()

        @pl.when(jnp.logical_and(nxt < 0, sweep + 1 < pl.num_programs(0)))
        def _():
            for cp in copies(te_ref[0], sweep + 1, 1 - slot):
                cp.start()

        for k, dst in enumerate(dsts):
            for r0 in range(0, dst.shape[0], _W_CAST_ROWS):
                dst[r0:r0 + _W_CAST_ROWS, :] = wbuf[slot, k, r0:r0 + _W_CAST_ROWS, :].astype(BF16)
        slot_ref[0] = 1 - slot


def _moe_up_kernel(te_ref, nx_ref, nu_ref, x_ref, w1_hbm, w3_hbm, o_ref, w1s, w3s, wbuf, sem, slot_ref, *, layer):
    i = pl.program_id(1)
    _expert_weights(te_ref, nx_ref, (w1_hbm, w3_hbm), wbuf, sem, slot_ref, (w1s, w3s), layer=layer,
                    col_tile=o_ref.shape[1])

    @pl.when(i < nu_ref[0])
    def _():
        x = x_ref[...]
        g = jnp.dot(x, w1s[...], preferred_element_type=F32)
        u = jnp.dot(x, w3s[...], preferred_element_type=F32)
        o_ref[...] = (_silu(g) * u).astype(o_ref.dtype)

    @pl.when(i >= nu_ref[0])
    def _():
        o_ref[...] = jnp.zeros_like(o_ref)


def _moe_down_kernel(te_ref, nx_ref, nu_ref, h_ref, rw_ref, w2_hbm, o_ref, w2s, wbuf, sem, slot_ref, *, layer):
    i = pl.program_id(1)
    _expert_weights(te_ref, nx_ref, (w2_hbm,), wbuf, sem, slot_ref, (w2s,), layer=layer, col_tile=o_ref.shape[1])

    @pl.when(i < nu_ref[0])
    def _():
        o_ref[...] = jnp.dot(h_ref[...], w2s[...], preferred_element_type=F32) * rw_ref[...]

    @pl.when(i >= nu_ref[0])
    def _():
        o_ref[...] = jnp.zeros_like(o_ref)


def _moe_experts(xs, w1, w3, w2, e, tile_expert, next_expert, n_used, row_w, *, tm, tf, tn):
    P, D = xs.shape
    F = w1.shape[3]
    assert P % tm == 0 and F % tf == 0 and D % tn == 0 and D % _W_CAST_ROWS == 0 and F % _W_CAST_ROWS == 0
    n_tiles = P // tm
    hbm = pl.BlockSpec(memory_space=pl.ANY)
    hmid = pl.pallas_call(
        functools.partial(_moe_up_kernel, layer=e),
        out_shape=jax.ShapeDtypeStruct((P, F), BF16),
        grid_spec=pltpu.PrefetchScalarGridSpec(
            num_scalar_prefetch=3, grid=(F // tf, n_tiles),
            in_specs=[pl.BlockSpec((tm, D), lambda f, i, te, nx, nu: (i, 0)), hbm, hbm],
            out_specs=pl.BlockSpec((tm, tf), lambda f, i, te, nx, nu: (i, f)),
            scratch_shapes=[pltpu.VMEM((D, tf), BF16), pltpu.VMEM((D, tf), BF16),
                            pltpu.VMEM((2, 2, D, tf), F32), pltpu.SemaphoreType.DMA((2,)),
                            pltpu.SMEM((1,), jnp.int32)]),
        compiler_params=_params(("arbitrary", "arbitrary")),
        name="moe_up",
    )(tile_expert, next_expert, n_used, xs, w1, w3)
    return pl.pallas_call(
        functools.partial(_moe_down_kernel, layer=e),
        out_shape=jax.ShapeDtypeStruct((P, D), F32),
        grid_spec=pltpu.PrefetchScalarGridSpec(
            num_scalar_prefetch=3, grid=(D // tn, n_tiles),
            in_specs=[pl.BlockSpec((tm, F), lambda n, i, te, nx, nu: (i, 0)),
                      pl.BlockSpec((tm, 1), lambda n, i, te, nx, nu: (i, 0)), hbm],
            out_specs=pl.BlockSpec((tm, tn), lambda n, i, te, nx, nu: (i, n)),
            scratch_shapes=[pltpu.VMEM((F, tn), BF16), pltpu.VMEM((2, 1, F, tn), F32),
                            pltpu.SemaphoreType.DMA((2,)), pltpu.SMEM((1,), jnp.int32)]),
        compiler_params=_params(("arbitrary", "arbitrary")),
        name="moe_down",
    )(tile_expert, next_expert, n_used, hmid, row_w, w2)


def _pack_bf16_pairs(h):
    w = h.shape[1] // 2
    hb = h.astype(BF16).astype(F32)
    hi = lax.bitcast_convert_type(hb[:, :w], jnp.uint32)
    lo = lax.bitcast_convert_type(hb[:, w:], jnp.uint32)
    return hi | (lo >> 16)


def _unpack_bf16_pairs(u):
    hi = lax.bitcast_convert_type(u & jnp.uint32(0xFFFF0000), F32).astype(BF16)
    lo = lax.bitcast_convert_type(u << 16, F32).astype(BF16)
    return hi, lo


_DMA_UNROLL = 8


def _wait_rows(src_hbm, dst, sem):
    pltpu.make_async_copy(src_hbm.at[pl.ds(0, dst.shape[0]), :], dst, sem).wait()


def _gather_kernel(src_ref, h_hbm, o_ref, buf, sem, *, rows):
    i = pl.program_id(0)
    n = pl.num_programs(0)

    def issue(tile):
        slot = tile % 2

        def body(r, carry):
            pltpu.make_async_copy(h_hbm.at[pl.ds(src_ref[tile * rows + r], 1), :],
                                  buf.at[slot, pl.ds(r, 1), :], sem.at[slot]).start()
            return carry

        lax.fori_loop(0, rows, body, 0, unroll=_DMA_UNROLL)

    @pl.when(i == 0)
    def _():
        issue(i)

    @pl.when(i + 1 < n)
    def _():
        issue(i + 1)

    slot = i % 2
    _wait_rows(h_hbm, buf.at[slot], sem.at[slot])
    w = buf.shape[2]
    hi, lo = _unpack_bf16_pairs(buf[slot])
    o_ref[:, :w] = hi
    o_ref[:, w:] = lo


def _gather_rows(h_packed, row_src, n_rows):
    T, W = h_packed.shape
    rows = 256
    return pl.pallas_call(
        functools.partial(_gather_kernel, rows=rows),
        out_shape=jax.ShapeDtypeStruct((n_rows, 2 * W), BF16),
        grid_spec=pltpu.PrefetchScalarGridSpec(
            num_scalar_prefetch=1, grid=(n_rows // rows,),
            in_specs=[pl.BlockSpec(memory_space=pl.ANY)],
            out_specs=pl.BlockSpec((rows, 2 * W), lambda i, src: (i, 0)),
            scratch_shapes=[pltpu.VMEM((2, rows, W), jnp.uint32), pltpu.SemaphoreType.DMA((2,))]),
        compiler_params=_params(("arbitrary",)),
        name="moe_gather",
    )(row_src, h_packed)


def _combine_kernel(pos_ref, x_ref, gt_ref, fg_ref, ys_hbm, o_ref, buf, sem, *, rows, top_k):
    i = pl.program_id(0)
    n = pl.num_programs(0)

    def issue(tile):
        slot = tile % 2

        def body(r, carry):
            for s in range(top_k):
                pltpu.make_async_copy(ys_hbm.at[pl.ds(pos_ref[(tile * rows + r) * top_k + s], 1), :],
                                      buf.at[slot, s, pl.ds(r, 1), :], sem.at[slot]).start()
            return carry

        lax.fori_loop(0, rows, body, 0, unroll=_DMA_UNROLL)

    @pl.when(i == 0)
    def _():
        issue(i)

    @pl.when(i + 1 < n)
    def _():
        issue(i + 1)

    slot = i % 2
    for s in range(top_k):
        _wait_rows(ys_hbm, buf.at[slot, s], sem.at[slot])
    y = buf[slot, 0]
    for s in range(1, top_k):
        y = y + buf[slot, s]
    x = x_ref[...] + gt_ref[0] * y
    o_ref[...] = _rms(x, fg_ref[...])


def _combine(pos, x, gate, final_g, ys, seq, top_k):
    T, D = x.shape
    B = gate.shape[0]
    rows = 256
    return pl.pallas_call(
        functools.partial(_combine_kernel, rows=rows, top_k=top_k),
        out_shape=jax.ShapeDtypeStruct((T, D), F32),
        grid_spec=pltpu.PrefetchScalarGridSpec(
            num_scalar_prefetch=1, grid=(T // rows,),
            in_specs=[pl.BlockSpec((rows, D), lambda i, p: (i, 0)),
                      pl.BlockSpec((1, 1, D), lambda i, p: (i * rows // seq, 0, 0)),
                      pl.BlockSpec((1, D), lambda i, p: (0, 0)),
                      pl.BlockSpec(memory_space=pl.ANY)],
            out_specs=pl.BlockSpec((rows, D), lambda i, p: (i, 0)),
            scratch_shapes=[pltpu.VMEM((2, top_k, rows, D), F32), pltpu.SemaphoreType.DMA((2,))]),
        compiler_params=_params(("arbitrary",)),
        name="moe_combine",
    )(pos, x, gate.reshape(B, 1, D), final_g.reshape(1, D), ys)


def _route(top_i, top_w, tm, n_tiles):
    T, top_k = top_i.shape
    flat_e = top_i.reshape(-1)
    onehot = (flat_e[:, None] == jnp.arange(N_EXPERTS, dtype=jnp.int32)[None, :]).astype(jnp.int32)
    rank = jnp.sum((jnp.cumsum(onehot, axis=0) - onehot) * onehot, axis=1)
    counts = jnp.sum(onehot, axis=0)
    padded = (counts + tm - 1) // tm * tm
    ends = jnp.cumsum(padded)
    starts = ends - padded
    pos = starts[flat_e] + rank
    token = jnp.arange(T * top_k, dtype=jnp.int32) // top_k
    row_src = jnp.zeros((n_tiles * tm,), jnp.int32).at[pos].set(token)
    row_w = jnp.zeros((n_tiles * tm,), F32).at[pos].set(top_w.reshape(-1))
    tile_start = jnp.arange(n_tiles, dtype=jnp.int32) * tm
    tile_expert = jnp.minimum(jnp.sum((tile_start[:, None] >= ends[None, :]).astype(jnp.int32), axis=1),
                              N_EXPERTS - 1)
    n_used = (ends[-1] // tm).reshape(1)
    tile_id = jnp.arange(n_tiles, dtype=jnp.int32)
    first = jnp.logical_and(tile_id < n_used[0],
                            jnp.concatenate([jnp.ones((1,), bool), tile_expert[1:] != tile_expert[:-1]]))
    first_pos = jnp.where(first, tile_id, n_tiles)
    later_first = jnp.concatenate([lax.cummin(first_pos, reverse=True)[1:], jnp.full((1,), n_tiles, jnp.int32)])
    next_expert = jnp.where(later_first < n_tiles, tile_expert[jnp.minimum(later_first, n_tiles - 1)], -1)
    return (pos.astype(jnp.int32), row_src, row_w.reshape(-1, 1), tile_expert.astype(jnp.int32),
            next_expert.astype(jnp.int32), n_used.astype(jnp.int32))


def _q_up_weight(w):
    r = w.shape[0]
    w = w.reshape(r, MLA_HEADS, MLA_QK)
    nope, rp = w[:, :, :MLA_NOPE], w[:, :, MLA_NOPE:]
    half = MLA_ROPE // 2
    rot = jnp.concatenate([-rp[:, :, half:], rp[:, :, :half]], axis=2)
    return jnp.concatenate([nope, rp, rot], axis=2).reshape(r, -1).astype(BF16)


def _kv_up_weight(w):
    r = w.shape[0]
    w = w.reshape(r, MLA_HEADS, MLA_NOPE + MLA_V)
    return jnp.concatenate([w[:, :, :MLA_NOPE].reshape(r, -1), w[:, :, MLA_NOPE:].reshape(r, -1)],
                           axis=1).astype(BF16)


def _rope_tables(positions):
    pos = positions.reshape(-1).astype(F32)[:, None]
    inv_m = ROPE_BASE ** (-jnp.arange(0, MLA_ROPE, 2, dtype=F32) / MLA_ROPE)
    am = pos * inv_m
    zm = jnp.zeros((pos.shape[0], V7X_LANES - MLA_ROPE), F32)
    cos_m = jnp.concatenate([jnp.cos(am), jnp.cos(am), zm], axis=1)
    sin_m = jnp.concatenate([jnp.sin(am), jnp.sin(am), zm], axis=1)
    inv_r = ROPE_BASE ** (-jnp.arange(0, RET_DK, 2, dtype=F32) / RET_DK)
    ar = pos * inv_r
    cos_r = jnp.concatenate([jnp.cos(ar), jnp.cos(ar)], axis=1)
    sin_r = jnp.concatenate([-jnp.sin(ar), jnp.sin(ar)], axis=1)
    return cos_m, sin_m, cos_r, sin_r


def kernel(x, c, positions, w_ada, b_ada, ada_table, norm_g, w_in, mla_q_norm, mla_w_q_up, mla_kv_norm,
           mla_w_kv_up, hg_lb_logits, hg_norm, ret_norm, w_branch, w_out, ffn_w1, ffn_w3, ffn_w2,
           moe_router, moe_w1, moe_w3, moe_w2, final_norm):
    B, S, D = x.shape
    T = B * S
    depth = norm_g.shape[0]
    top_k = 2
    assert depth % 2 == 0, "the final norm is fused into the expert combine of the last (odd) layer"
    assert S % 512 == 0 and D % 512 == 0

    mod = _ada(c, w_ada, b_ada, ada_table)
    cos_m, sin_m, cos_r, sin_r = _rope_tables(positions)

    n_cqkv = MLA_Q_RANK + MLA_KV_RANK
    n_lat = n_cqkv + MLA_ROPE
    n_hg = 4 * HG_WIDTH
    n_ret = 2 * RET_HEADS * RET_DK + 2 * RET_HEADS * RET_DV

    d_in = w_in.shape[2]
    w_in_t = jnp.swapaxes(w_in, 1, 2).reshape(depth, d_in // _WT_GROUP, _WT_GROUP, D)

    xf = x.reshape(T, D)
    pending = None
    out = None
    for l in range(depth):
        shift1, scale1, gate1, shift2, scale2, gate2 = (mod[l, :, j] for j in range(N_MOD))
        if pending is None:
            (h,) = _norm(xf, norm_g[l, 0], scale1, shift1, S)
        else:
            xf, h = _norm(xf, norm_g[l, 0], scale1, shift1, S, resid=pending)
            pending = None

        lat = _matmul(h, w_in_t, l, 0, n_cqkv, transposed=True)
        kr_raw = _matmul(h, w_in_t, l, n_cqkv, V7X_LANES, transposed=True, tn=V7X_LANES)
        hg = _matmul(h, w_in_t, l, n_lat, n_hg, transposed=True)
        rt = _matmul(h, w_in_t, l, n_lat + n_hg, n_ret, transposed=True)
        gates = _matmul(h, w_in_t, l, n_lat + n_hg + n_ret, N_BRANCH * D, transposed=True, mode="sigmoid",
                        out_dtype=BF16)

        y_mla = _mla(lat, kr_raw, cos_m, sin_m, mla_q_norm[l], _q_up_weight(mla_w_q_up[l]),
                     mla_kv_norm[l], _kv_up_weight(mla_w_kv_up[l]), B, S)
        y_hg = _hgrn2(hg, hg_lb_logits, hg_norm[l], l, B, S)
        y_ret = _retention(rt, cos_r, sin_r, ret_norm[l], B, S)

        merged = _merge(y_mla, y_hg, y_ret, w_branch, l, gates)
        xf = _matmul(merged, w_out, l, 0, D, mode="resid", resid=xf, gate=gate1, seq=S)

        if l % 2 == 0:
            (h2,) = _norm(xf, norm_g[l, 1], scale2, shift2, S)
            e = l // 2
            y = _ffn(h2, ffn_w1.astype(BF16), ffn_w3.astype(BF16), ffn_w2.astype(BF16), e, tm=512, tf=256)
            pending = (y, gate2)
        else:
            e = l // 2
            h2, ti, tw = _norm(xf, norm_g[l, 1], scale2, shift2, S, router=moe_router[e])
            tm = 512
            n_tiles = T * top_k // tm + N_EXPERTS
            pos, row_src, row_w, tile_expert, next_expert, n_used = _route(ti[:, :top_k], tw[:, :top_k], tm,
                                                                           n_tiles)
            xs = _gather_rows(h2, row_src, n_tiles * tm)
            ys = _moe_experts(xs, moe_w1, moe_w3, moe_w2, e, tile_expert, next_expert, n_used, row_w, tm=tm,
                              tf=512, tn=1024)
            assert l == depth - 1
            out = _combine(pos, xf, gate2, final_norm, ys, S, top_k)
    return out.reshape(B, S, D)
```

```python
import functools

import jax
import jax.numpy as jnp
from jax import lax
from jax.experimental import pallas as pl
from jax.experimental.pallas import tpu as pltpu

F32 = jnp.float32
BF16 = jnp.bfloat16

V7X_LANES = 128
V7X_SUBLANES = 8
V7X_VMEM_BYTES = 64 * 1024 * 1024
V7X_VMEM_LIMIT = V7X_VMEM_BYTES - 8 * 1024 * 1024

EPS = 1e-6
LOG2_E = 1.4426950408889634
ROPE_BASE = 10000.0
MASK_VALUE = -1e30

MLA_HEADS = 16
MLA_NOPE = 128
MLA_ROPE = 64
MLA_V = 128
MLA_Q_RANK = 1024
MLA_KV_RANK = 512
MLA_QK = MLA_NOPE + MLA_ROPE
MLA_QPAD = 2 * V7X_LANES

HG_HEADS = 16
HG_D = 128
HG_WIDTH = HG_HEADS * HG_D
HG_CHUNK = 64
HG_SUB = 8
HG_MIN_F = 1e-6

RET_HEADS = 8
RET_DK = 128
RET_DV = 256
RET_CHUNK = 128

N_BRANCH = 3
N_EXPERTS = 8
N_MOD = 6


def _params(semantics):
    return pltpu.CompilerParams(dimension_semantics=semantics, vmem_limit_bytes=V7X_VMEM_LIMIT)


def _rms(x, gain):
    return x * lax.rsqrt(jnp.mean(x * x, axis=-1, keepdims=True) + EPS) * gain


def _silu(x):
    return x * jax.nn.sigmoid(x)


def _dot_nt(a, b):
    return lax.dot_general(a, b, (((1,), (1,)), ((), ())), preferred_element_type=F32)


def _dot_tn(a, b):
    return jnp.dot(a.T, b, preferred_element_type=F32)


def _ada_kernel(c_ref, w_ref, b_ref, tab_ref, o_ref, acc_ref):
    k = pl.program_id(0)

    @pl.when(k == 0)
    def _():
        acc_ref[...] = jnp.zeros_like(acc_ref)

    s = _silu(c_ref[...]).astype(BF16)
    acc_ref[...] += jnp.dot(s, w_ref[...].astype(BF16), preferred_element_type=F32)

    @pl.when(k == pl.num_programs(0) - 1)
    def _():
        acc = acc_ref[...] + b_ref[...]
        for l in range(o_ref.shape[0]):
            o_ref[l] = acc + tab_ref[l]


def _ada(c, w_ada, b_ada, ada_table):
    B, D = c.shape
    depth = ada_table.shape[0]
    N = w_ada.shape[1]
    tk = V7X_LANES
    cp = jnp.zeros((V7X_SUBLANES, D), F32).at[:B].set(c)
    out = pl.pallas_call(
        _ada_kernel,
        out_shape=jax.ShapeDtypeStruct((depth, V7X_SUBLANES, N), F32),
        grid=(D // tk,),
        in_specs=[
            pl.BlockSpec((V7X_SUBLANES, tk), lambda k: (0, k)),
            pl.BlockSpec((tk, N), lambda k: (k, 0)),
            pl.BlockSpec((1, N), lambda k: (0, 0)),
            pl.BlockSpec((depth, 1, N), lambda k: (0, 0, 0)),
        ],
        out_specs=pl.BlockSpec((depth, V7X_SUBLANES, N), lambda k: (0, 0, 0)),
        scratch_shapes=[pltpu.VMEM((V7X_SUBLANES, N), F32)],
        compiler_params=_params(("arbitrary",)),
        name="ada_mod",
    )(cp, w_ada, b_ada.reshape(1, N), ada_table.reshape(depth, 1, N))
    return out[:, :B].reshape(depth, B, N_MOD, D)


def _norm_kernel(*refs, has_resid, has_router, n_experts):
    it = iter(refs)
    x_ref = next(it)
    if has_resid:
        y_ref, gt_ref = next(it), next(it)
    g_ref, sc_ref, sh_ref = next(it), next(it), next(it)
    if has_router:
        r_ref = next(it)
    if has_resid:
        xo_ref = next(it)
    h_ref = next(it)
    if has_router:
        ti_ref, tw_ref = next(it), next(it)

    x = x_ref[...]
    if has_resid:
        x = x + gt_ref[0] * y_ref[...]
        xo_ref[...] = x
    h = _rms(x, g_ref[...]) * (1.0 + sc_ref[0]) + sh_ref[0]
    if not has_router:
        h_ref[...] = h.astype(h_ref.dtype)
    else:
        h_ref[...] = _pack_bf16_pairs(h)
        logits = jnp.dot(h, r_ref[...], preferred_element_type=F32, precision=lax.Precision.HIGHEST)
        lane = lax.broadcasted_iota(jnp.int32, logits.shape, 1).astype(F32)
        neg = jnp.float32(-jnp.inf)
        big = jnp.float32(V7X_LANES)
        lg = jnp.where(lane < n_experts, logits, neg)
        m1 = jnp.max(lg, axis=-1, keepdims=True)
        i1 = jnp.min(jnp.where(lg == m1, lane, big), axis=-1, keepdims=True)
        lg2 = jnp.where(lane == i1, neg, lg)
        m2 = jnp.max(lg2, axis=-1, keepdims=True)
        i2 = jnp.min(jnp.where(lg2 == m2, lane, big), axis=-1, keepdims=True)
        e = jnp.exp(m2 - m1)
        w1 = 1.0 / (1.0 + e)
        w2 = e / (1.0 + e)
        ti_ref[...] = jnp.where(lane == 0.0, i1, jnp.where(lane == 1.0, i2, 0.0)).astype(jnp.int32)
        tw_ref[...] = jnp.where(lane == 0.0, w1, jnp.where(lane == 1.0, w2, 0.0))


def _norm(x, gain, scale, shift, seq, *, resid=None, router=None):
    T, D = x.shape
    B = scale.shape[0]
    tm = 256
    row = pl.BlockSpec((tm, D), lambda i: (i, 0))
    per_batch = pl.BlockSpec((1, 1, D), lambda i: (i * tm // seq, 0, 0))
    args, in_specs = [x], [row]
    if resid is not None:
        args += [resid[0], resid[1].reshape(B, 1, D)]
        in_specs += [row, per_batch]
    args += [gain.reshape(1, D), scale.reshape(B, 1, D), shift.reshape(B, 1, D)]
    in_specs += [pl.BlockSpec((1, D), lambda i: (0, 0)), per_batch, per_batch]
    out_shape, out_specs = [], []
    if router is not None:
        n_experts = router.shape[1]
        rp = jnp.zeros((D, V7X_LANES), F32).at[:, :n_experts].set(router)
        args.append(rp)
        in_specs.append(pl.BlockSpec((D, V7X_LANES), lambda i: (0, 0)))
    else:
        n_experts = 0
    if resid is not None:
        out_shape.append(jax.ShapeDtypeStruct((T, D), F32))
        out_specs.append(row)
    if router is None:
        out_shape.append(jax.ShapeDtypeStruct((T, D), BF16))
        out_specs.append(row)
    else:
        out_shape.append(jax.ShapeDtypeStruct((T, D // 2), jnp.uint32))
        out_specs.append(pl.BlockSpec((tm, D // 2), lambda i: (i, 0)))
    if router is not None:
        small = pl.BlockSpec((tm, V7X_LANES), lambda i: (i, 0))
        out_shape += [jax.ShapeDtypeStruct((T, V7X_LANES), jnp.int32), jax.ShapeDtypeStruct((T, V7X_LANES), F32)]
        out_specs += [small, small]
    return pl.pallas_call(
        functools.partial(_norm_kernel, has_resid=resid is not None, has_router=router is not None,
                          n_experts=n_experts),
        out_shape=out_shape,
        grid=(T // tm,),
        in_specs=in_specs,
        out_specs=out_specs,
        compiler_params=_params(("arbitrary",)),
        name="ada_norm",
    )(*args)


_W_CAST_ROWS = 512


_WT_GROUP = 64


def _mm_kernel(*refs, mode, n_groups):
    it = iter(refs)
    a_ref = next(it)
    w_refs = [next(it) for _ in range(max(n_groups, 1))]
    if mode == "resid":
        r_ref, g_ref = next(it), next(it)
    o_ref, wscr = next(it), next(it)
    c = pl.program_id(2)

    @pl.when(pl.program_id(1) == 0)
    def _():
        if n_groups:
            for g, w_ref in enumerate(w_refs):
                wscr[c, g * _WT_GROUP:(g + 1) * _WT_GROUP, :] = w_ref[0, 0].astype(BF16)
        else:
            for r0 in range(0, wscr.shape[1], _W_CAST_ROWS):
                wscr[c, r0:r0 + _W_CAST_ROWS, :] = w_refs[0][0, r0:r0 + _W_CAST_ROWS, :].astype(BF16)

    if n_groups:
        acc = _dot_nt(a_ref[...], wscr[c])
    else:
        acc = jnp.dot(a_ref[...], wscr[c], preferred_element_type=F32)
    if mode == "plain":
        o_ref[...] = acc.astype(o_ref.dtype)
    elif mode == "sigmoid":
        o_ref[...] = jax.nn.sigmoid(acc).astype(o_ref.dtype)
    else:
        o_ref[...] = r_ref[...] + g_ref[0] * acc


def _matmul(a, w, layer, col0, n, *, transposed=False, mode="plain", out_dtype=F32, resid=None, gate=None,
            seq=None, tm=1024, tn=512):
    M, K = a.shape
    assert n % tn == 0 and M % tm == 0
    pair = 2 if (n // tn) % 2 == 0 else 1

    def col_tile(jj, i, c):
        return jj * pair + jnp.where(i == 0, c, pair - 1)

    args = [a]
    in_specs = [pl.BlockSpec((tm, K), lambda jj, i, c: (i, 0))]
    if transposed:
        assert col0 % _WT_GROUP == 0 and tn % _WT_GROUP == 0
        n_groups = tn // _WT_GROUP
        g0 = col0 // _WT_GROUP
        for g in range(n_groups):
            args.append(w)
            in_specs.append(pl.BlockSpec(
                (1, 1, _WT_GROUP, K),
                lambda jj, i, c, g=g: (layer, g0 + col_tile(jj, i, c) * n_groups + g, 0, 0)))
        wscr = pltpu.VMEM((pair, tn, K), BF16)
    else:
        assert col0 % tn == 0 and K % _W_CAST_ROWS == 0
        n_groups = 0
        args.append(w)
        in_specs.append(pl.BlockSpec((1, K, tn), lambda jj, i, c: (layer, 0, col0 // tn + col_tile(jj, i, c))))
        wscr = pltpu.VMEM((pair, K, tn), BF16)
    out_spec = pl.BlockSpec((tm, tn), lambda jj, i, c: (i, jj * pair + c))
    if mode == "resid":
        B = gate.shape[0]
        args += [resid, gate.reshape(B, 1, n)]
        in_specs += [out_spec, pl.BlockSpec((1, 1, tn), lambda jj, i, c: (i * tm // seq, 0, jj * pair + c))]
    return pl.pallas_call(
        functools.partial(_mm_kernel, mode=mode, n_groups=n_groups),
        out_shape=jax.ShapeDtypeStruct((M, n), out_dtype),
        grid=(n // (tn * pair), M // tm, pair),
        in_specs=in_specs,
        out_specs=out_spec,
        scratch_shapes=[wscr],
        compiler_params=_params(("arbitrary", "arbitrary", "arbitrary")),
        name="matmul_" + mode,
    )(*args)


def _mla_q_kernel(cq_ref, g_ref, w_ref, cos_ref, sin_ref, o_ref, a_scr, *, heads_per_step, scale):
    @pl.when(pl.program_id(1) == 0)
    def _():
        a_scr[...] = _rms(cq_ref[...], g_ref[...]).astype(BF16)

    acc = jnp.dot(a_scr[...], w_ref[...], preferred_element_type=F32)
    cos, sin = cos_ref[...], sin_ref[...]
    L = V7X_LANES
    for h in range(heads_per_step):
        nope = acc[:, 2 * L * h:2 * L * h + L]
        rp = acc[:, 2 * L * h + L:2 * L * h + 2 * L]
        o_ref[:, 2 * L * h:2 * L * h + L] = (nope * scale).astype(BF16)
        roped = rp * cos + pltpu.roll(rp, MLA_ROPE, 1) * sin
        o_ref[:, 2 * L * h + L:2 * L * h + 2 * L] = (roped * scale).astype(BF16)


def _mla_kv_kernel(ckv_ref, g_ref, w_ref, kraw_ref, cos_ref, sin_ref, kv_ref, kr_ref, a_scr):
    @pl.when(pl.program_id(1) == 0)
    def _():
        a_scr[...] = _rms(ckv_ref[...], g_ref[...]).astype(BF16)
        x = kraw_ref[...]
        half = MLA_ROPE // 2
        lane = lax.broadcasted_iota(jnp.int32, x.shape, 1)
        swapped = jnp.where(lane < half, -pltpu.roll(x, V7X_LANES - half, 1), pltpu.roll(x, half, 1))
        roped = x * cos_ref[...] + swapped * sin_ref[...]
        kr_ref[...] = jnp.where(lane < MLA_ROPE, roped, 0.0).astype(BF16)

    kv_ref[...] = jnp.dot(a_scr[...], w_ref[...], preferred_element_type=F32).astype(BF16)


def _mla_attn_kernel(q_ref, k_ref, kr_ref, v_ref, o_ref, *, tq, n_heads):
    i = pl.program_id(2)
    qs = [q_ref[:, h * MLA_QPAD:(h + 1) * MLA_QPAD] for h in range(n_heads)]

    ones = jnp.ones((tq, V7X_LANES), BF16)

    def block(j, carry, masked):
        off = pl.multiple_of(j * tq, tq)
        kr = kr_ref[pl.ds(off, tq), :]
        scores = []
        for h in range(n_heads):
            kk = jnp.concatenate([k_ref[pl.ds(off, tq), h * MLA_NOPE:(h + 1) * MLA_NOPE], kr], axis=1)
            scores.append(_dot_nt(qs[h], kk))
        out = []
        for h in range(n_heads):
            m, acc = carry[h]
            s = scores[h]
            if masked:
                r = lax.broadcasted_iota(jnp.int32, s.shape, 0)
                c = lax.broadcasted_iota(jnp.int32, s.shape, 1)
                s = jnp.where(c <= r, s, MASK_VALUE)
            m_new = jnp.maximum(m, jnp.max(s, axis=-1, keepdims=True))
            p = jnp.exp2(s - m_new).astype(BF16)
            v1 = jnp.concatenate([v_ref[pl.ds(off, tq), h * MLA_V:(h + 1) * MLA_V], ones], axis=1)
            acc = jnp.exp2(m - m_new) * acc + jnp.dot(p, v1, preferred_element_type=F32)
            out.append((m_new, acc))
        return tuple(out)

    carry = tuple((jnp.full((tq, 1), MASK_VALUE, F32), jnp.zeros((tq, MLA_V + V7X_LANES), F32))
                  for _ in range(n_heads))
    carry = lax.fori_loop(0, i, lambda j, c: block(j, c, False), carry)
    carry = block(i, carry, True)
    for h in range(n_heads):
        _, acc = carry[h]
        o_ref[:, h * MLA_V:(h + 1) * MLA_V] = (acc[:, :MLA_V] / acc[:, MLA_V:]).astype(o_ref.dtype)


def _mla(lat, kr_raw, cos_m, sin_m, q_norm, wq, kv_norm, wkv, B, S):
    T = lat.shape[0]
    L = V7X_LANES
    tm = 512
    hps = 4
    scale = MLA_QK ** -0.5 * LOG2_E
    q = pl.pallas_call(
        functools.partial(_mla_q_kernel, heads_per_step=hps, scale=scale),
        out_shape=jax.ShapeDtypeStruct((T, MLA_HEADS * MLA_QPAD), BF16),
        grid=(T // tm, MLA_HEADS // hps),
        in_specs=[
            pl.BlockSpec((tm, MLA_Q_RANK), lambda i, j: (i, 0)),
            pl.BlockSpec((1, MLA_Q_RANK), lambda i, j: (0, 0)),
            pl.BlockSpec((MLA_Q_RANK, hps * MLA_QPAD), lambda i, j: (0, j)),
            pl.BlockSpec((tm, L), lambda i, j: (i, 0)),
            pl.BlockSpec((tm, L), lambda i, j: (i, 0)),
        ],
        out_specs=pl.BlockSpec((tm, hps * MLA_QPAD), lambda i, j: (i, j)),
        scratch_shapes=[pltpu.VMEM((tm, MLA_Q_RANK), BF16)],
        compiler_params=_params(("arbitrary", "arbitrary")),
        name="mla_q",
    )(lat, q_norm.reshape(1, -1), wq, cos_m, sin_m)

    tn = 1024
    kv_w = 2 * MLA_HEADS * MLA_NOPE
    ckv_blk = MLA_Q_RANK // MLA_KV_RANK
    kv, kr = pl.pallas_call(
        _mla_kv_kernel,
        out_shape=[jax.ShapeDtypeStruct((T, kv_w), BF16), jax.ShapeDtypeStruct((T, L), BF16)],
        grid=(T // tm, kv_w // tn),
        in_specs=[
            pl.BlockSpec((tm, MLA_KV_RANK), lambda i, j: (i, ckv_blk)),
            pl.BlockSpec((1, MLA_KV_RANK), lambda i, j: (0, 0)),
            pl.BlockSpec((MLA_KV_RANK, tn), lambda i, j: (0, j)),
            pl.BlockSpec((tm, L), lambda i, j: (i, 0)),
            pl.BlockSpec((tm, L), lambda i, j: (i, 0)),
            pl.BlockSpec((tm, L), lambda i, j: (i, 0)),
        ],
        out_specs=[pl.BlockSpec((tm, tn), lambda i, j: (i, j)), pl.BlockSpec((tm, L), lambda i, j: (i, 0))],
        scratch_shapes=[pltpu.VMEM((tm, MLA_KV_RANK), BF16)],
        compiler_params=_params(("arbitrary", "arbitrary")),
        name="mla_kv",
    )(lat, kv_norm.reshape(1, -1), wkv, kr_raw, cos_m, sin_m)

    tq = 512
    nq = S // tq
    nh = 4
    hg = MLA_HEADS // nh
    return pl.pallas_call(
        functools.partial(_mla_attn_kernel, tq=tq, n_heads=nh),
        out_shape=jax.ShapeDtypeStruct((T, MLA_HEADS * MLA_V), BF16),
        grid=(B, hg, nq),
        in_specs=[
            pl.BlockSpec((tq, nh * MLA_QPAD), lambda b, h, i: (b * nq + i, h)),
            pl.BlockSpec((S, nh * MLA_NOPE), lambda b, h, i: (b, h)),
            pl.BlockSpec((S, L), lambda b, h, i: (b, 0)),
            pl.BlockSpec((S, nh * MLA_V), lambda b, h, i: (b, hg + h)),
        ],
        out_specs=pl.BlockSpec((tq, nh * MLA_V), lambda b, h, i: (b * nq + i, h)),
        compiler_params=_params(("arbitrary", "arbitrary", "arbitrary")),
        name="mla_attn",
    )(q, kv, kr, kv)


def _hgrn2_kernel(q_ref, f_ref, i_ref, gate_ref, lbl_ref, ng_ref, o_ref, st_scr, q_scr, k_scr, b_scr, *, layer):
    tb = q_ref.shape[0]
    C = HG_CHUNK

    @pl.when(pl.program_id(2) == 0)
    def _():
        st_scr[...] = jnp.zeros_like(st_scr)

    lg = lbl_ref[...]
    e = jnp.exp(lg - jnp.max(lg, axis=0, keepdims=True))
    p = e / jnp.sum(e, axis=0, keepdims=True)
    lb = jnp.zeros((1, HG_D), F32)
    for l in range(1, layer + 1):
        lb = lb + p[l:l + 1, :]

    fr = f_ref[...]
    f = lb + (1.0 - lb) * jax.nn.sigmoid(fr)
    logf = jnp.log(jnp.maximum(f, HG_MIN_F))
    q_scr[...] = _silu(q_ref[...])
    k_scr[...] = (1.0 - lb) * jax.nn.sigmoid(-fr)

    hi = logf.astype(BF16)
    r1 = logf - hi.astype(F32)
    mid = r1.astype(BF16)
    lo = (r1 - mid.astype(F32)).astype(BF16)
    rr = lax.broadcasted_iota(jnp.int32, (C, C), 0)
    cc = lax.broadcasted_iota(jnp.int32, (C, C), 1)
    causal = cc <= rr
    tril = jnp.where(causal, 1.0, 0.0).astype(BF16)
    for c in range(tb // C):
        sl = slice(c * C, (c + 1) * C)
        parts = jnp.concatenate([hi[sl], mid[sl], lo[sl]], axis=1)
        cs = jnp.dot(tril, parts, preferred_element_type=F32)
        b_scr[sl, :] = cs[:, :HG_D] + cs[:, HG_D:2 * HG_D] + cs[:, 2 * HG_D:]

    intra, incr = [], []
    for c in range(tb // C):
        r0 = c * C
        b_c = b_scr[r0:r0 + C, :]
        k_c = k_scr[r0:r0 + C, :]
        v_c = i_ref[r0:r0 + C, :]
        b_last = b_scr[r0 + C - 1:r0 + C, :]
        rows = []
        for blk in range(C // HG_SUB):
            s0 = r0 + blk * HG_SUB
            n = (blk + 1) * HG_SUB
            ref = b_scr[s0 + HG_SUB // 2 - 1:s0 + HG_SUB // 2, :]
            q_b = q_scr[s0:s0 + HG_SUB, :] * jnp.exp(b_scr[s0:s0 + HG_SUB, :] - ref)
            k_b = k_scr[r0:r0 + n, :] * jnp.exp(ref - b_scr[r0:r0 + n, :])
            if n < C:
                k_b = jnp.concatenate([k_b, jnp.zeros((C - n, HG_D), F32)], axis=0)
            rows.append(_dot_nt(q_b, k_b))
        scores = jnp.where(causal, jnp.concatenate(rows, axis=0), 0.0)
        intra.append(jnp.dot(scores, v_c, preferred_element_type=F32))
        incr.append(_dot_tn(v_c, k_c * jnp.exp(b_last - b_c)))

    st = st_scr[...]
    ng = ng_ref[...]
    for c in range(tb // C):
        r0 = c * C
        b_c = b_scr[r0:r0 + C, :]
        b_last = b_scr[r0 + C - 1:r0 + C, :]
        o = intra[c] + _dot_nt(q_scr[r0:r0 + C, :] * jnp.exp(b_c), st)
        st = st * jnp.exp(b_last) + incr[c]
        y = _rms(o, ng) * _silu(gate_ref[r0:r0 + C, :])
        o_ref[r0:r0 + C, :] = y.astype(o_ref.dtype)
    st_scr[...] = st


def _hgrn2(hg, lb_logits, norm_g, layer, B, S):
    T = hg.shape[0]
    tb = 512
    nb = S // tb
    depth = lb_logits.shape[0]

    def part(p):
        return pl.BlockSpec((tb, HG_D), lambda b, h, j: (b * nb + j, p * HG_HEADS + h))

    return pl.pallas_call(
        functools.partial(_hgrn2_kernel, layer=layer),
        out_shape=jax.ShapeDtypeStruct((T, HG_WIDTH), BF16),
        grid=(B, HG_HEADS, nb),
        in_specs=[part(0), part(1), part(2), part(3),
                  pl.BlockSpec((depth, HG_D), lambda b, h, j: (0, h)),
                  pl.BlockSpec((1, HG_D), lambda b, h, j: (0, h))],
        out_specs=pl.BlockSpec((tb, HG_D), lambda b, h, j: (b * nb + j, h)),
        scratch_shapes=[pltpu.VMEM((HG_D, HG_D), F32), pltpu.VMEM((tb, HG_D), F32),
                        pltpu.VMEM((tb, HG_D), F32), pltpu.VMEM((tb, HG_D), F32)],
        compiler_params=_params(("arbitrary", "arbitrary", "arbitrary")),
        name="hgrn2",
    )(hg, hg, hg, hg, lb_logits, norm_g.reshape(1, -1))


def _ret_kernel(q_ref, k_ref, v_ref, gate_ref, cos_ref, sin_ref, lg_ref, ng_ref, o_ref, st_scr):
    tb = q_ref.shape[0]
    C = RET_CHUNK

    @pl.when(pl.program_id(2) == 0)
    def _():
        st_scr[...] = jnp.zeros_like(st_scr)

    cos, sin = cos_ref[...], sin_ref[...]
    half = RET_DK // 2
    xq, xk = q_ref[...], k_ref[...]
    q = xq * cos + pltpu.roll(xq, half, 1) * sin
    k = (xk * cos + pltpu.roll(xk, half, 1) * sin) * (RET_DK ** -0.5)

    lg = lg_ref[0]
    rr = lax.broadcasted_iota(jnp.int32, (C, C), 0)
    cc = lax.broadcasted_iota(jnp.int32, (C, C), 1)
    rel = (rr - cc).astype(F32)
    decay = jnp.where(rel >= 0.0, jnp.exp(lg * jnp.maximum(rel, 0.0)), 0.0)
    idx = rr.astype(F32)
    q_dec = jnp.exp(lg * (idx + 1.0))
    k_dec = jnp.exp(lg * (C - 1.0 - idx))
    chunk_dec = jnp.exp(lg * float(C))

    intra, incr = [], []
    for c in range(tb // C):
        sl = slice(c * C, (c + 1) * C)
        q_c, k_c = q[sl], k[sl]
        v_c = v_ref[sl, :]
        s = _dot_nt(q_c.astype(BF16), k_c.astype(BF16)) * decay
        intra.append(jnp.dot(s.astype(BF16), v_c.astype(BF16), preferred_element_type=F32))
        incr.append(jnp.dot(v_c.T.astype(BF16), (k_c * k_dec).astype(BF16), preferred_element_type=F32))

    st = st_scr[...]
    ng = ng_ref[...]
    for c in range(tb // C):
        sl = slice(c * C, (c + 1) * C)
        o = intra[c] + _dot_nt((q[sl] * q_dec).astype(BF16), st.astype(BF16))
        st = st * chunk_dec + incr[c]
        y = _rms(o, ng) * _silu(gate_ref[sl, :])
        o_ref[sl, :] = y.astype(o_ref.dtype)
    st_scr[...] = st


def _retention(rt, cos_r, sin_r, norm_g, B, S):
    T = rt.shape[0]
    tb = 512
    nb = S // tb
    H = RET_HEADS
    v_blk0 = 2 * H * RET_DK // RET_DV
    log_gamma = jnp.log1p(-(2.0 ** (-5.0 - jnp.arange(H, dtype=F32))))
    lg = jnp.broadcast_to(log_gamma[:, None, None], (H, 1, RET_CHUNK))
    return pl.pallas_call(
        _ret_kernel,
        out_shape=jax.ShapeDtypeStruct((T, H * RET_DV), BF16),
        grid=(B, H, nb),
        in_specs=[
            pl.BlockSpec((tb, RET_DK), lambda b, h, j: (b * nb + j, h)),
            pl.BlockSpec((tb, RET_DK), lambda b, h, j: (b * nb + j, H + h)),
            pl.BlockSpec((tb, RET_DV), lambda b, h, j: (b * nb + j, v_blk0 + h)),
            pl.BlockSpec((tb, RET_DV), lambda b, h, j: (b * nb + j, v_blk0 + H + h)),
            pl.BlockSpec((tb, RET_DK), lambda b, h, j: (b * nb + j, 0)),
            pl.BlockSpec((tb, RET_DK), lambda b, h, j: (b * nb + j, 0)),
            pl.BlockSpec((1, 1, RET_CHUNK), lambda b, h, j: (h, 0, 0)),
            pl.BlockSpec((1, RET_DV), lambda b, h, j: (0, h)),
        ],
        out_specs=pl.BlockSpec((tb, RET_DV), lambda b, h, j: (b * nb + j, h)),
        scratch_shapes=[pltpu.VMEM((RET_DV, RET_DK), F32)],
        compiler_params=_params(("arbitrary", "arbitrary", "arbitrary")),
        name="retention",
    )(rt, rt, rt, rt, cos_r, sin_r, lg, norm_g.reshape(1, -1))


def _merge_kernel(ya_ref, yb_ref, yc_ref, w_ref, ga_ref, gb_ref, gc_ref, o_ref, wscr):
    @pl.when(pl.program_id(1) == 0)
    def _():
        for br in range(N_BRANCH):
            for r0 in range(0, wscr.shape[1], _W_CAST_ROWS):
                wscr[br, r0:r0 + _W_CAST_ROWS, :] = w_ref[0, br, r0:r0 + _W_CAST_ROWS, :].astype(BF16)

    acc = ga_ref[...].astype(F32) * jnp.dot(ya_ref[...], wscr[0], preferred_element_type=F32)
    acc = acc + gb_ref[...].astype(F32) * jnp.dot(yb_ref[...], wscr[1], preferred_element_type=F32)
    acc = acc + gc_ref[...].astype(F32) * jnp.dot(yc_ref[...], wscr[2], preferred_element_type=F32)
    o_ref[...] = acc.astype(o_ref.dtype)


def _merge(ya, yb, yc, w_branch, layer, gates):
    T, K = ya.shape
    D = w_branch.shape[3]
    tm, tn = 512, 512
    nj = D // tn
    ybs = pl.BlockSpec((tm, K), lambda j, i: (i, 0))
    return pl.pallas_call(
        _merge_kernel,
        out_shape=jax.ShapeDtypeStruct((T, D), BF16),
        grid=(nj, T // tm),
        in_specs=[ybs, ybs, ybs,
                  pl.BlockSpec((1, N_BRANCH, K, tn), lambda j, i: (layer, 0, 0, j)),
                  pl.BlockSpec((tm, tn), lambda j, i: (i, j)),
                  pl.BlockSpec((tm, tn), lambda j, i: (i, nj + j)),
                  pl.BlockSpec((tm, tn), lambda j, i: (i, 2 * nj + j))],
        out_specs=pl.BlockSpec((tm, tn), lambda j, i: (i, j)),
        scratch_shapes=[pltpu.VMEM((N_BRANCH, K, tn), BF16)],
        compiler_params=_params(("arbitrary", "arbitrary")),
        name="merge",
    )(ya, yb, yc, w_branch, gates, gates, gates)


def _ffn_kernel(x_ref, w1_ref, w3_ref, w2_ref, o_ref):
    @pl.when(pl.program_id(1) == 0)
    def _():
        o_ref[...] = jnp.zeros_like(o_ref)

    x = x_ref[...]
    g = jnp.dot(x, w1_ref[0], preferred_element_type=F32)
    u = jnp.dot(x, w3_ref[0], preferred_element_type=F32)
    o_ref[...] += jnp.dot((_silu(g) * u).astype(BF16), w2_ref[0], preferred_element_type=F32)


def _ffn(x, w1, w3, w2, e, *, tm, tf):
    M, D = x.shape
    F = w1.shape[2]
    assert M % tm == 0 and F % tf == 0
    w13 = pl.BlockSpec((1, D, tf), lambda i, f: (e, 0, f))
    return pl.pallas_call(
        _ffn_kernel,
        out_shape=jax.ShapeDtypeStruct((M, D), F32),
        grid=(M // tm, F // tf),
        in_specs=[pl.BlockSpec((tm, D), lambda i, f: (i, 0)), w13, w13,
                  pl.BlockSpec((1, tf, D), lambda i, f: (e, f, 0))],
        out_specs=pl.BlockSpec((tm, D), lambda i, f: (i, 0)),
        compiler_params=_params(("arbitrary", "arbitrary")),
        name="ffn_dense",
    )(x, w1, w3, w2)


def _expert_changed(te_ref, i):
    return jnp.logical_or(i == 0, te_ref[i] != te_ref[jnp.maximum(i - 1, 0)])


def _expert_weights(te_ref, nx_ref, w_hbms, wbuf, sem, slot_ref, dsts, *, layer, col_tile):
    sweep = pl.program_id(0)
    i = pl.program_id(1)

    def copies(expert, swp, slot):
        col = pl.multiple_of(swp * col_tile, col_tile)
        return [pltpu.make_async_copy(w.at[layer, expert, :, pl.ds(col, col_tile)], wbuf.at[slot, k], sem.at[slot])
                for k, w in enumerate(w_hbms)]

    @pl.when(_expert_changed(te_ref, i))
    def _():
        @pl.when(jnp.logical_and(sweep == 0, i == 0))
        def _():
            slot_ref[0] = 0
            for cp in copies(te_ref[0], 0, 0):
                cp.start()

        slot = slot_ref[0]
        for cp in copies(te_ref[i], sweep, slot):
            cp.wait()
        nxt = nx_ref[i]

        @pl.when(nxt >= 0)
        def _():
            for cp in copies(nxt, sweep, 1 - slot):
                cp.start()

        @pl.when(jnp.logical_and(nxt < 0, sweep + 1 < pl.num_programs(0)))
        def _():
            for cp in copies(te_ref[0], sweep + 1, 1 - slot):
                cp.start()

        for k, dst in enumerate(dsts):
            for r0 in range(0, dst.shape[0], _W_CAST_ROWS):
                dst[r0:r0 + _W_CAST_ROWS, :] = wbuf[slot, k, r0:r0 + _W_CAST_ROWS, :].astype(BF16)
        slot_ref[0] = 1 - slot


def _moe_up_kernel(te_ref, nx_ref, nu_ref, x_ref, w1_hbm, w3_hbm, o_ref, w1s, w3s, wbuf, sem, slot_ref, *, layer):
    i = pl.program_id(1)
    _expert_weights(te_ref, nx_ref, (w1_hbm, w3_hbm), wbuf, sem, slot_ref, (w1s, w3s), layer=layer,
                    col_tile=o_ref.shape[1])

    @pl.when(i < nu_ref[0])
    def _():
        x = x_ref[...]
        g = jnp.dot(x, w1s[...], preferred_element_type=F32)
        u = jnp.dot(x, w3s[...], preferred_element_type=F32)
        o_ref[...] = (_silu(g) * u).astype(o_ref.dtype)

    @pl.when(i >= nu_ref[0])
    def _():
        o_ref[...] = jnp.zeros_like(o_ref)


def _moe_down_kernel(te_ref, nx_ref, nu_ref, h_ref, w2_hbm, o_ref, w2s, wbuf, sem, slot_ref, *, layer):
    i = pl.program_id(1)
    _expert_weights(te_ref, nx_ref, (w2_hbm,), wbuf, sem, slot_ref, (w2s,), layer=layer, col_tile=o_ref.shape[1])

    @pl.when(i < nu_ref[0])
    def _():
        o_ref[...] = jnp.dot(h_ref[...], w2s[...], preferred_element_type=F32)

    @pl.when(i >= nu_ref[0])
    def _():
        o_ref[...] = jnp.zeros_like(o_ref)


def _moe_experts(xs, w1, w3, w2, e, tile_expert, next_expert, n_used, *, tm, tf, tn):
    P, D = xs.shape
    F = w1.shape[3]
    assert P % tm == 0 and F % tf == 0 and D % tn == 0 and D % _W_CAST_ROWS == 0 and F % _W_CAST_ROWS == 0
    n_tiles = P // tm
    hbm = pl.BlockSpec(memory_space=pl.ANY)
    hmid = pl.pallas_call(
        functools.partial(_moe_up_kernel, layer=e),
        out_shape=jax.ShapeDtypeStruct((P, F), BF16),
        grid_spec=pltpu.PrefetchScalarGridSpec(
            num_scalar_prefetch=3, grid=(F // tf, n_tiles),
            in_specs=[pl.BlockSpec((tm, D), lambda f, i, te, nx, nu: (i, 0)), hbm, hbm],
            out_specs=pl.BlockSpec((tm, tf), lambda f, i, te, nx, nu: (i, f)),
            scratch_shapes=[pltpu.VMEM((D, tf), BF16), pltpu.VMEM((D, tf), BF16),
                            pltpu.VMEM((2, 2, D, tf), F32), pltpu.SemaphoreType.DMA((2,)),
                            pltpu.SMEM((1,), jnp.int32)]),
        compiler_params=_params(("arbitrary", "arbitrary")),
        name="moe_up",
    )(tile_expert, next_expert, n_used, xs, w1, w3)
    return pl.pallas_call(
        functools.partial(_moe_down_kernel, layer=e),
        out_shape=jax.ShapeDtypeStruct((P, D), F32),
        grid_spec=pltpu.PrefetchScalarGridSpec(
            num_scalar_prefetch=3, grid=(D // tn, n_tiles),
            in_specs=[pl.BlockSpec((tm, F), lambda n, i, te, nx, nu: (i, 0)), hbm],
            out_specs=pl.BlockSpec((tm, tn), lambda n, i, te, nx, nu: (i, n)),
            scratch_shapes=[pltpu.VMEM((F, tn), BF16), pltpu.VMEM((2, 1, F, tn), F32),
                            pltpu.SemaphoreType.DMA((2,)), pltpu.SMEM((1,), jnp.int32)]),
        compiler_params=_params(("arbitrary", "arbitrary")),
        name="moe_down",
    )(tile_expert, next_expert, n_used, hmid, w2)


def _pack_bf16_pairs(h):
    w = h.shape[1] // 2
    hb = h.astype(BF16).astype(F32)
    hi = lax.bitcast_convert_type(hb[:, :w], jnp.uint32)
    lo = lax.bitcast_convert_type(hb[:, w:], jnp.uint32)
    return hi | (lo >> 16)


def _unpack_bf16_pairs(u):
    hi = lax.bitcast_convert_type(u & jnp.uint32(0xFFFF0000), F32).astype(BF16)
    lo = lax.bitcast_convert_type(u << 16, F32).astype(BF16)
    return hi, lo


_DMA_UNROLL = 8


def _wait_rows(src_hbm, dst, sem):
    pltpu.make_async_copy(src_hbm.at[pl.ds(0, dst.shape[0]), :], dst, sem).wait()


def _gather_kernel(src_ref, h_hbm, o_ref, buf, sem, *, rows):
    i = pl.program_id(0)
    n = pl.num_programs(0)

    def issue(tile):
        slot = tile % 2

        def body(r, carry):
            pltpu.make_async_copy(h_hbm.at[pl.ds(src_ref[tile * rows + r], 1), :],
                                  buf.at[slot, pl.ds(r, 1), :], sem.at[slot]).start()
            return carry

        lax.fori_loop(0, rows, body, 0, unroll=_DMA_UNROLL)

    @pl.when(i == 0)
    def _():
        issue(i)

    @pl.when(i + 1 < n)
    def _():
        issue(i + 1)

    slot = i % 2
    _wait_rows(h_hbm, buf.at[slot], sem.at[slot])
    w = buf.shape[2]
    hi, lo = _unpack_bf16_pairs(buf[slot])
    o_ref[:, :w] = hi
    o_ref[:, w:] = lo


def _gather_rows(h_packed, row_src, n_rows):
    T, W = h_packed.shape
    rows = 256
    return pl.pallas_call(
        functools.partial(_gather_kernel, rows=rows),
        out_shape=jax.ShapeDtypeStruct((n_rows, 2 * W), BF16),
        grid_spec=pltpu.PrefetchScalarGridSpec(
            num_scalar_prefetch=1, grid=(n_rows // rows,),
            in_specs=[pl.BlockSpec(memory_space=pl.ANY)],
            out_specs=pl.BlockSpec((rows, 2 * W), lambda i, src: (i, 0)),
            scratch_shapes=[pltpu.VMEM((2, rows, W), jnp.uint32), pltpu.SemaphoreType.DMA((2,))]),
        compiler_params=_params(("arbitrary",)),
        name="moe_gather",
    )(row_src, h_packed)


def _combine_kernel(pos_ref, x_ref, tw_ref, gt_ref, fg_ref, ys_hbm, o_ref, buf, sem, *, rows, top_k):
    i = pl.program_id(0)
    n = pl.num_programs(0)

    def issue(tile):
        slot = tile % 2

        def body(r, carry):
            for s in range(top_k):
                pltpu.make_async_copy(ys_hbm.at[pl.ds(pos_ref[(tile * rows + r) * top_k + s], 1), :],
                                      buf.at[slot, s, pl.ds(r, 1), :], sem.at[slot]).start()
            return carry

        lax.fori_loop(0, rows, body, 0, unroll=_DMA_UNROLL)

    @pl.when(i == 0)
    def _():
        issue(i)

    @pl.when(i + 1 < n)
    def _():
        issue(i + 1)

    slot = i % 2
    for s in range(top_k):
        _wait_rows(ys_hbm, buf.at[slot, s], sem.at[slot])
    tw = tw_ref[...]
    y = tw[:, 0:1] * buf[slot, 0]
    for s in range(1, top_k):
        y = y + tw[:, s:s + 1] * buf[slot, s]
    x = x_ref[...] + gt_ref[0] * y
    o_ref[...] = _rms(x, fg_ref[...])


def _combine(pos, x, top_w, gate, final_g, ys, seq, top_k):
    T, D = x.shape
    B = gate.shape[0]
    rows = 256
    return pl.pallas_call(
        functools.partial(_combine_kernel, rows=rows, top_k=top_k),
        out_shape=jax.ShapeDtypeStruct((T, D), F32),
        grid_spec=pltpu.PrefetchScalarGridSpec(
            num_scalar_prefetch=1, grid=(T // rows,),
            in_specs=[pl.BlockSpec((rows, D), lambda i, p: (i, 0)),
                      pl.BlockSpec((rows, V7X_LANES), lambda i, p: (i, 0)),
                      pl.BlockSpec((1, 1, D), lambda i, p: (i * rows // seq, 0, 0)),
                      pl.BlockSpec((1, D), lambda i, p: (0, 0)),
                      pl.BlockSpec(memory_space=pl.ANY)],
            out_specs=pl.BlockSpec((rows, D), lambda i, p: (i, 0)),
            scratch_shapes=[pltpu.VMEM((2, top_k, rows, D), F32), pltpu.SemaphoreType.DMA((2,))]),
        compiler_params=_params(("arbitrary",)),
        name="moe_combine",
    )(pos, x, top_w, gate.reshape(B, 1, D), final_g.reshape(1, D), ys)


def _route(top_i, tm, n_tiles):
    T, top_k = top_i.shape
    flat_e = top_i.reshape(-1)
    onehot = (flat_e[:, None] == jnp.arange(N_EXPERTS, dtype=jnp.int32)[None, :]).astype(jnp.int32)
    rank = jnp.sum((jnp.cumsum(onehot, axis=0) - onehot) * onehot, axis=1)
    counts = jnp.sum(onehot, axis=0)
    padded = (counts + tm - 1) // tm * tm
    ends = jnp.cumsum(padded)
    starts = ends - padded
    pos = starts[flat_e] + rank
    token = jnp.arange(T * top_k, dtype=jnp.int32) // top_k
    row_src = jnp.zeros((n_tiles * tm,), jnp.int32).at[pos].set(token)
    tile_start = jnp.arange(n_tiles, dtype=jnp.int32) * tm
    tile_expert = jnp.minimum(jnp.sum((tile_start[:, None] >= ends[None, :]).astype(jnp.int32), axis=1),
                              N_EXPERTS - 1)
    n_used = (ends[-1] // tm).reshape(1)
    tile_id = jnp.arange(n_tiles, dtype=jnp.int32)
    first = jnp.logical_and(tile_id < n_used[0],
                            jnp.concatenate([jnp.ones((1,), bool), tile_expert[1:] != tile_expert[:-1]]))
    first_pos = jnp.where(first, tile_id, n_tiles)
    later_first = jnp.concatenate([lax.cummin(first_pos, reverse=True)[1:], jnp.full((1,), n_tiles, jnp.int32)])
    next_expert = jnp.where(later_first < n_tiles, tile_expert[jnp.minimum(later_first, n_tiles - 1)], -1)
    return (pos.astype(jnp.int32), row_src, tile_expert.astype(jnp.int32), next_expert.astype(jnp.int32),
            n_used.astype(jnp.int32))


def _q_up_weight(w):
    r = w.shape[0]
    w = w.reshape(r, MLA_HEADS, MLA_QK)
    nope, rp = w[:, :, :MLA_NOPE], w[:, :, MLA_NOPE:]
    half = MLA_ROPE // 2
    rot = jnp.concatenate([-rp[:, :, half:], rp[:, :, :half]], axis=2)
    return jnp.concatenate([nope, rp, rot], axis=2).reshape(r, -1).astype(BF16)


def _kv_up_weight(w):
    r = w.shape[0]
    w = w.reshape(r, MLA_HEADS, MLA_NOPE + MLA_V)
    return jnp.concatenate([w[:, :, :MLA_NOPE].reshape(r, -1), w[:, :, MLA_NOPE:].reshape(r, -1)],
                           axis=1).astype(BF16)


def _rope_tables(positions):
    pos = positions.reshape(-1).astype(F32)[:, None]
    inv_m = ROPE_BASE ** (-jnp.arange(0, MLA_ROPE, 2, dtype=F32) / MLA_ROPE)
    am = pos * inv_m
    zm = jnp.zeros((pos.shape[0], V7X_LANES - MLA_ROPE), F32)
    cos_m = jnp.concatenate([jnp.cos(am), jnp.cos(am), zm], axis=1)
    sin_m = jnp.concatenate([jnp.sin(am), jnp.sin(am), zm], axis=1)
    inv_r = ROPE_BASE ** (-jnp.arange(0, RET_DK, 2, dtype=F32) / RET_DK)
    ar = pos * inv_r
    cos_r = jnp.concatenate([jnp.cos(ar), jnp.cos(ar)], axis=1)
    sin_r = jnp.concatenate([-jnp.sin(ar), jnp.sin(ar)], axis=1)
    return cos_m, sin_m, cos_r, sin_r


def kernel(x, c, positions, w_ada, b_ada, ada_table, norm_g, w_in, mla_q_norm, mla_w_q_up, mla_kv_norm,
           mla_w_kv_up, hg_lb_logits, hg_norm, ret_norm, w_branch, w_out, ffn_w1, ffn_w3, ffn_w2,
           moe_router, moe_w1, moe_w3, moe_w2, final_norm):
    B, S, D = x.shape
    T = B * S
    depth = norm_g.shape[0]
    top_k = 2
    assert depth % 2 == 0, "the final norm is fused into the expert combine of the last (odd) layer"
    assert S % 512 == 0 and D % 512 == 0

    mod = _ada(c, w_ada, b_ada, ada_table)
    cos_m, sin_m, cos_r, sin_r = _rope_tables(positions)

    n_cqkv = MLA_Q_RANK + MLA_KV_RANK
    n_lat = n_cqkv + MLA_ROPE
    n_hg = 4 * HG_WIDTH
    n_ret = 2 * RET_HEADS * RET_DK + 2 * RET_HEADS * RET_DV

    d_in = w_in.shape[2]
    w_in_t = jnp.swapaxes(w_in, 1, 2).reshape(depth, d_in // _WT_GROUP, _WT_GROUP, D)

    xf = x.reshape(T, D)
    pending = None
    out = None
    for l in range(depth):
        shift1, scale1, gate1, shift2, scale2, gate2 = (mod[l, :, j] for j in range(N_MOD))
        if pending is None:
            (h,) = _norm(xf, norm_g[l, 0], scale1, shift1, S)
        else:
            xf, h = _norm(xf, norm_g[l, 0], scale1, shift1, S, resid=pending)
            pending = None

        lat = _matmul(h, w_in_t, l, 0, n_cqkv, transposed=True)
        kr_raw = _matmul(h, w_in_t, l, n_cqkv, V7X_LANES, transposed=True, tn=V7X_LANES)
        hg = _matmul(h, w_in_t, l, n_lat, n_hg, transposed=True)
        rt = _matmul(h, w_in_t, l, n_lat + n_hg, n_ret, transposed=True)
        gates = _matmul(h, w_in_t, l, n_lat + n_hg + n_ret, N_BRANCH * D, transposed=True, mode="sigmoid",
                        out_dtype=BF16)

        y_mla = _mla(lat, kr_raw, cos_m, sin_m, mla_q_norm[l], _q_up_weight(mla_w_q_up[l]),
                     mla_kv_norm[l], _kv_up_weight(mla_w_kv_up[l]), B, S)
        y_hg = _hgrn2(hg, hg_lb_logits, hg_norm[l], l, B, S)
        y_ret = _retention(rt, cos_r, sin_r, ret_norm[l], B, S)

        merged = _merge(y_mla, y_hg, y_ret, w_branch, l, gates)
        xf = _matmul(merged, w_out, l, 0, D, mode="resid", resid=xf, gate=gate1, seq=S)

        if l % 2 == 0:
            (h2,) = _norm(xf, norm_g[l, 1], scale2, shift2, S)
            e = l // 2
            y = _ffn(h2, ffn_w1.astype(BF16), ffn_w3.astype(BF16), ffn_w2.astype(BF16), e, tm=512, tf=256)
            pending = (y, gate2)
        else:
            e = l // 2
            h2, ti, tw = _norm(xf, norm_g[l, 1], scale2, shift2, S, router=moe_router[e])
            tm = 512
            n_tiles = T * top_k // tm + N_EXPERTS
            pos, row_src, tile_expert, next_expert, n_used = _route(ti[:, :top_k], tm, n_tiles)
            xs = _gather_rows(h2, row_src, n_tiles * tm)
            ys = _moe_experts(xs, moe_w1, moe_w3, moe_w2, e, tile_expert, next_expert, n_used, tm=tm, tf=512,
                              tn=1024)
            assert l == depth - 1
            out = _combine(pos, xf, tw, gate2, final_norm, ys, S, top_k)
    return out.reshape(B, S, D)
```
